```python
import math
import jax
import jax.numpy as jnp
from jax import lax
import numpy as np

D_MODEL = 1024
BATCH = 2
SEQ = 8192
DEPTH = 2
DEC_BATCH = 32
DEC_SEQ = 4
PAST_LEN = 16384
PAGE_SIZE = 128

HEAD_DIM = 64
N_HEADS = 8
N_KV_HEADS = 2
GROUP = N_HEADS // N_KV_HEADS
BRANCH_W = N_HEADS * HEAD_DIM
KV_W = N_KV_HEADS * HEAD_DIM
N_BRANCH = 3
IDX_HEADS = 4
IDX_DIM = 64
DSA_TOPK = 256
MOBA_BLOCK = 256
MOBA_TOPK = 3
Q_BLOCK = 128
ROPE_THETA = 10000.0
RMS_EPS = 1e-6
ATTN_SCALE = HEAD_DIM ** -0.5
IDX_SCALE = (IDX_HEADS ** -0.5) * (IDX_DIM ** -0.5)
FORGET_BIAS_MEAN = 4.0

IN_SPLITS = (
    ('a_q', BRANCH_W), ('a_k', KV_W), ('a_v', KV_W), ('a_z', BRANCH_W),
    ('a_iq', IDX_HEADS * IDX_DIM), ('a_ik', IDX_DIM), ('a_iw', IDX_HEADS),
    ('b_q', BRANCH_W), ('b_k', KV_W), ('b_v', KV_W), ('b_z', BRANCH_W),
    ('c_q', BRANCH_W), ('c_k', KV_W), ('c_v', KV_W), ('c_z', BRANCH_W), ('c_f', N_HEADS),
    ('gates', N_BRANCH * D_MODEL),
)
IN_WIDTH = sum(w for _, w in IN_SPLITS)

kernel_name = "gated_dsa_moba_fox_hybrid_step"


def rms_norm(x, g):
    xf = x.astype(jnp.float32)
    y = xf * lax.rsqrt(jnp.mean(xf * xf, axis=-1, keepdims=True) + RMS_EPS)
    return (y * g.astype(jnp.float32)).astype(x.dtype)


def rope(x, pos):
    half = x.shape[-1] // 2
    freqs = ROPE_THETA ** (-jnp.arange(half, dtype=jnp.float32) / half)
    ang = pos.astype(jnp.float32)[:, None] * freqs[None, :]
    cos = jnp.cos(ang)[:, None, :]
    sin = jnp.sin(ang)[:, None, :]
    xf = x.astype(jnp.float32)
    x1, x2 = xf[..., :half], xf[..., half:]
    return jnp.concatenate([x1 * cos - x2 * sin, x1 * sin + x2 * cos], axis=-1).astype(x.dtype)


def project(h, w_in_l):
    p = h @ w_in_l
    parts = {}
    off = 0
    for name, width in IN_SPLITS:
        parts[name] = p[..., off:off + width]
        off += width
    return parts


def masked_softmax(logits, mask):
    return jax.nn.softmax(jnp.where(mask, logits.astype(jnp.float32), -jnp.inf), axis=-1)


def dsa_attend(pos_q, q, qi, wi, k, v, ki, n_sel):
    B, Tq = q.shape[:2]
    L = k.shape[1]
    rel = jax.nn.relu(jnp.einsum('bqhd,bsd->bqhs', qi, ki).astype(jnp.float32))
    score = jnp.einsum('bqh,bqhs->bqs', wi.astype(jnp.float32), rel) * IDX_SCALE
    causal = jnp.arange(L)[None, :] <= pos_q[:, None]
    score = jnp.where(causal[None], score, -jnp.inf)
    _, sel = lax.top_k(score, n_sel)
    bi = jnp.arange(B)[:, None, None]
    ks = k[bi, sel]
    vs = v[bi, sel]
    qg = q.reshape(B, Tq, N_KV_HEADS, GROUP, HEAD_DIM)
    logits = jnp.einsum('btngd,btjnd->btngj', qg, ks) * ATTN_SCALE
    valid = (sel <= pos_q[None, :, None])[:, :, None, None, :]
    p = masked_softmax(logits, valid).astype(v.dtype)
    o = jnp.einsum('btngj,btjnd->btngd', p, vs)
    return o.reshape(B, Tq, BRANCH_W)


def moba_blocks(k, v):
    B, L = k.shape[:2]
    nb = -(-L // MOBA_BLOCK)
    pad = nb * MOBA_BLOCK - L

    def blk(a):
        a = jnp.pad(a, ((0, 0), (0, pad), (0, 0), (0, 0)))
        return a.reshape(B, nb, MOBA_BLOCK, N_KV_HEADS, HEAD_DIM).transpose(0, 3, 1, 2, 4)

    kb, vb = blk(k), blk(v)
    kmean = jnp.mean(kb.astype(jnp.float32), axis=3).astype(k.dtype)
    return kb, vb, kmean


def moba_attend(pos_q, q, kb, vb, kmean):
    B, Tq = q.shape[:2]
    nb = kb.shape[2]
    n_sel = min(MOBA_TOPK, nb)
    qg = q.reshape(B, Tq, N_KV_HEADS, GROUP, HEAD_DIM)
    own = pos_q // MOBA_BLOCK
    gate = jnp.einsum('btngd,bnjd->btngj', qg, kmean).astype(jnp.float32)
    past = (jnp.arange(nb)[None, :] < own[:, None])[None, :, None, None, :]
    gate = jnp.where(past, gate, -jnp.inf)
    _, sel = lax.top_k(gate, n_sel)
    sel_valid = sel < own[None, :, None, None, None]
    bi = jnp.arange(B)[:, None, None, None, None]
    ni = jnp.arange(N_KV_HEADS)[None, None, :, None, None]
    ks = kb[bi, ni, sel]
    vs = vb[bi, ni, sel]
    lp = jnp.einsum('btngd,btngjsd->btngjs', qg, ks).astype(jnp.float32) * ATTN_SCALE
    lp = jnp.where(sel_valid[..., None], lp, -jnp.inf).reshape(B, Tq, N_KV_HEADS, GROUP, n_sel * MOBA_BLOCK)
    ko = kb[:, :, own]
    vo = vb[:, :, own]
    lo = jnp.einsum('btngd,bntsd->btngs', qg, ko).astype(jnp.float32) * ATTN_SCALE
    own_pos = own[:, None] * MOBA_BLOCK + jnp.arange(MOBA_BLOCK)[None, :]
    lo = jnp.where((own_pos <= pos_q[:, None])[None, :, None, None, :], lo, -jnp.inf)
    p = jax.nn.softmax(jnp.concatenate([lp, lo], axis=-1), axis=-1).astype(vb.dtype)
    pp = p[..., :n_sel * MOBA_BLOCK].reshape(B, Tq, N_KV_HEADS, GROUP, n_sel, MOBA_BLOCK)
    po = p[..., n_sel * MOBA_BLOCK:]
    o = (jnp.einsum('btngjs,btngjsd->btngd', pp, vs)
         + jnp.einsum('btngs,bntsd->btngd', po, vo))
    return o.reshape(B, Tq, BRANCH_W)


def fox_attend(pos_q, q, cq, k, v, ck):
    B, Tq = q.shape[:2]
    L = k.shape[1]
    qg = q.reshape(B, Tq, N_KV_HEADS, GROUP, HEAD_DIM)
    logits = jnp.einsum('btngd,bsnd->btngs', qg, k).astype(jnp.float32) * ATTN_SCALE
    bias = (cq.reshape(B, Tq, N_KV_HEADS, GROUP)[..., None]
            - ck.reshape(B, L, N_KV_HEADS, GROUP).transpose(0, 2, 3, 1)[:, None])
    causal = (jnp.arange(L)[None, :] <= pos_q[:, None])[None, :, None, None, :]
    p = masked_softmax(logits + bias, causal).astype(v.dtype)
    o = jnp.einsum('btngs,bsnd->btngd', p, v)
    return o.reshape(B, Tq, BRANCH_W)


def query_blocks(fn, pos, *qs):
    T = pos.shape[0]
    nblk = T // Q_BLOCK

    def split(a):
        return a.reshape(a.shape[0], nblk, Q_BLOCK, *a.shape[2:]).swapaxes(0, 1)

    out = lax.map(lambda args: fn(*args), (pos.reshape(nblk, Q_BLOCK),) + tuple(split(a) for a in qs))
    return out.swapaxes(0, 1).reshape(qs[0].shape[0], T, *out.shape[3:])


def gather_pages(cache_l, page_table):
    g = cache_l[page_table]
    return g.reshape(g.shape[0], g.shape[1] * g.shape[2], *g.shape[3:])


def layer_forward(x, pos, past, g_norm, w_in_l, b_f_l, w_br_l, w_out_l, blocked):
    B, T, _ = x.shape
    h = rms_norm(x, g_norm)
    p = project(h, w_in_l)

    def heads(a, n):
        return a.reshape(B, T, n, -1)

    a_q = rope(heads(p['a_q'], N_HEADS), pos)
    a_k = rope(heads(p['a_k'], N_KV_HEADS), pos)
    a_v = heads(p['a_v'], N_KV_HEADS)
    a_iq = rope(heads(p['a_iq'], IDX_HEADS), pos)
    a_ik = rope(p['a_ik'][:, :, None, :], pos)[:, :, 0]
    a_iw = p['a_iw']
    b_q = rope(heads(p['b_q'], N_HEADS), pos)
    b_k = rope(heads(p['b_k'], N_KV_HEADS), pos)
    b_v = heads(p['b_v'], N_KV_HEADS)
    c_q = heads(p['c_q'], N_HEADS)
    c_k = heads(p['c_k'], N_KV_HEADS)
    c_v = heads(p['c_v'], N_KV_HEADS)
    logf = jax.nn.log_sigmoid(p['c_f'].astype(jnp.float32) + b_f_l.astype(jnp.float32)).astype(x.dtype)

    new_rows = (jnp.stack([a_k, a_v], axis=2), a_ik, jnp.stack([b_k, b_v], axis=2),
                jnp.stack([c_k, c_v], axis=2), logf)
    if past is None:
        akv, aik, bkv, ckv, lf = new_rows
    else:
        akv, aik, bkv, ckv, lf = (jnp.concatenate([pr, nr.astype(pr.dtype)], axis=1)
                                  for pr, nr in zip(past, new_rows))
    L = akv.shape[1]
    n_dsa = min(DSA_TOPK, L // 4)
    c_all = jnp.cumsum(lf.astype(jnp.float32), axis=1)
    c_q_cum = c_all[:, L - T:]
    kb, vb, kmean = moba_blocks(bkv[:, :, 0], bkv[:, :, 1])

    def fa(pos_b, q, qi, wi):
        return dsa_attend(pos_b, q, qi, wi, akv[:, :, 0], akv[:, :, 1], aik, n_dsa)

    def fb(pos_b, q):
        return moba_attend(pos_b, q, kb, vb, kmean)

    def fc(pos_b, q, cq):
        return fox_attend(pos_b, q, cq, ckv[:, :, 0], ckv[:, :, 1], c_all)

    if blocked:
        o_a = query_blocks(fa, pos, a_q, a_iq, a_iw)
        o_b = query_blocks(fb, pos, b_q)
        o_c = query_blocks(fc, pos, c_q, c_q_cum)
    else:
        o_a = fa(pos, a_q, a_iq, a_iw)
        o_b = fb(pos, b_q)
        o_c = fc(pos, c_q, c_q_cum)

    o = jnp.stack([o_a, o_b, o_c], axis=0)
    z = jnp.stack([p['a_z'], p['b_z'], p['c_z']], axis=0)
    u = jnp.einsum('nbtc,ncd->nbtd', o * jax.nn.silu(z), w_br_l)
    g = jax.nn.sigmoid(p['gates'].reshape(B, T, N_BRANCH, D_MODEL))
    m = jnp.einsum('btnd,nbtd->btd', g, u)
    return x + m @ w_out_l, new_rows


def setup_inputs(seed: int = 0) -> dict:
    key = jax.random.key(seed)
    ks = jax.random.split(key, 16)
    f32 = jnp.float32
    n_pages = PAST_LEN // PAGE_SIZE
    n_used = DEC_BATCH * n_pages
    n_pool = n_used + max(1, n_used // 4)
    perm = jax.random.permutation(ks[0], n_pool)
    page_table = perm[:n_used].reshape(DEC_BATCH, n_pages).astype(jnp.int32)
    kv_shape = (DEPTH, n_pool, PAGE_SIZE, 2, N_KV_HEADS, HEAD_DIM)
    x_prompt = jax.random.normal(ks[1], (BATCH, SEQ, D_MODEL), f32)
    x_sample = jax.random.normal(ks[2], (DEC_BATCH, DEC_SEQ, D_MODEL), f32)
    cache_a_kv = jax.random.normal(ks[3], kv_shape, f32)
    cache_a_idx = jax.random.normal(ks[4], (DEPTH, n_pool, PAGE_SIZE, IDX_DIM), f32)
    cache_b_kv = jax.random.normal(ks[5], kv_shape, f32)
    cache_c_kv = jax.random.normal(ks[6], kv_shape, f32)
    cache_c_logf = jax.nn.log_sigmoid(
        FORGET_BIAS_MEAN + jax.random.normal(ks[7], (DEPTH, n_pool, PAGE_SIZE, N_HEADS), f32))
    norm_g = 1.0 + 0.1 * jax.random.normal(ks[8], (DEPTH, D_MODEL), f32)
    w_in = jax.random.normal(ks[9], (DEPTH, D_MODEL, IN_WIDTH), f32) * D_MODEL ** -0.5
    b_forget = FORGET_BIAS_MEAN + jax.random.normal(ks[10], (DEPTH, N_HEADS), f32)
    w_branch = jax.random.normal(ks[11], (DEPTH, N_BRANCH, BRANCH_W, D_MODEL), f32) * BRANCH_W ** -0.5
    w_out = jax.random.normal(ks[12], (DEPTH, D_MODEL, D_MODEL), f32) * D_MODEL ** -0.5
    final_norm_g = 1.0 + 0.1 * jax.random.normal(ks[13], (D_MODEL,), f32)
    return {'x_prompt': x_prompt, 'x_sample': x_sample,
            'cache_a_kv': cache_a_kv, 'cache_a_idx': cache_a_idx, 'cache_b_kv': cache_b_kv,
            'cache_c_kv': cache_c_kv, 'cache_c_logf': cache_c_logf, 'page_table': page_table,
            'norm_g': norm_g, 'w_in': w_in, 'b_forget': b_forget, 'w_branch': w_branch,
            'w_out': w_out, 'final_norm_g': final_norm_g}


def reference(x_prompt, x_sample, cache_a_kv, cache_a_idx, cache_b_kv, cache_c_kv, cache_c_logf,
              page_table, norm_g, w_in, b_forget, w_branch, w_out, final_norm_g):
    past_len = page_table.shape[1] * cache_a_kv.shape[2]
    pos_p = jnp.arange(x_prompt.shape[1], dtype=jnp.int32)
    pos_s = past_len + jnp.arange(x_sample.shape[1], dtype=jnp.int32)
    xp, xs = x_prompt, x_sample
    rows_p, rows_s = [], []
    for l in range(DEPTH):
        xp, rp = layer_forward(xp, pos_p, None, norm_g[l], w_in[l], b_forget[l], w_branch[l], w_out[l], True)
        past = tuple(gather_pages(c[l], page_table)
                     for c in (cache_a_kv, cache_a_idx, cache_b_kv, cache_c_kv, cache_c_logf))
        xs, rs = layer_forward(xs, pos_s, past, norm_g[l], w_in[l], b_forget[l], w_branch[l], w_out[l], False)
        rows_p.append(rp)
        rows_s.append(rs)
    y_prompt = rms_norm(xp, final_norm_g)
    y_sample = rms_norm(xs, final_norm_g)
    p_a_kv, p_a_idx, p_b_kv, p_c_kv, p_c_logf = (jnp.stack([r[i] for r in rows_p]) for i in range(5))
    s_a_kv, s_a_idx, s_b_kv, s_c_kv, s_c_logf = (jnp.stack([r[i] for r in rows_s]) for i in range(5))
    return (y_prompt, y_sample, p_a_kv, p_a_idx, p_b_kv, p_c_kv, p_c_logf,
            s_a_kv, s_a_idx, s_b_kv, s_c_kv, s_c_logf)
```

```python
import functools

import numpy as np
import jax
import jax.numpy as jnp
from jax import lax
from jax.experimental import pallas as pl
from jax.experimental.pallas import tpu as pltpu

D_MODEL = 1024
HEAD_DIM = 64
N_HEADS = 8
N_KV_HEADS = 2
GROUP = N_HEADS // N_KV_HEADS
BRANCH_W = N_HEADS * HEAD_DIM
KV_W = N_KV_HEADS * HEAD_DIM
N_BRANCH = 3
IDX_HEADS = 4
IDX_DIM = 64
DSA_TOPK = 256
MOBA_BLOCK = 256
MOBA_TOPK = 3
ROPE_THETA = 10000.0
RMS_EPS = 1e-6
ATTN_SCALE = HEAD_DIM ** -0.5
IDX_SCALE = (IDX_HEADS ** -0.5) * (IDX_DIM ** -0.5)

LANES = 128
SUBLANES = 8
VMEM_LIMIT = 56 * 1024 * 1024
INT_MIN = -2 ** 31
NEG_INF = float("-inf")
F32 = jnp.float32
BF16 = jnp.bfloat16

IN_SPLITS = (
    ('a_q', BRANCH_W), ('a_k', KV_W), ('a_v', KV_W), ('a_z', BRANCH_W),
    ('a_iq', IDX_HEADS * IDX_DIM), ('a_ik', IDX_DIM), ('a_iw', IDX_HEADS),
    ('b_q', BRANCH_W), ('b_k', KV_W), ('b_v', KV_W), ('b_z', BRANCH_W),
    ('c_q', BRANCH_W), ('c_k', KV_W), ('c_v', KV_W), ('c_z', BRANCH_W), ('c_f', N_HEADS),
    ('gates', N_BRANCH * D_MODEL),
)
IN_OFF = {}
_o = 0
for _n, _w in IN_SPLITS:
    IN_OFF[_n] = _o
    _o += _w
IN_WIDTH = _o

MISC_F = 0
MISC_W = 8
AUG_ONE = 0
AUG_C = 3


def _wide_q_cols(name):
    src = np.full((N_HEADS * LANES,), -1, np.int64)
    for h in range(N_HEADS):
        g = h // GROUP
        dst = h * LANES + g * HEAD_DIM
        src[dst:dst + HEAD_DIM] = IN_OFF[name] + h * HEAD_DIM + np.arange(HEAD_DIM)
    return src


def _seg(name, width, pad_to=None):
    src = IN_OFF[name] + np.arange(width)
    if pad_to is not None and pad_to > width:
        src = np.concatenate([src, np.full((pad_to - width,), -1, np.int64)])
    return src


def _swap_halves(src):
    return np.concatenate([src[HEAD_DIM:2 * HEAD_DIM], src[:HEAD_DIM]])


def _build_layout():
    segs = []
    segs.append(('a_qw', _wide_q_cols('a_q')))
    segs.append(('b_qw', _wide_q_cols('b_q')))
    iq = np.full((IDX_HEADS * LANES,), -1, np.int64)
    for h in range(IDX_HEADS):
        iq[h * LANES:h * LANES + IDX_DIM] = IN_OFF['a_iq'] + h * IDX_DIM + np.arange(IDX_DIM)
    segs.append(('a_iqw', iq))
    segs.append(('a_k', _seg('a_k', KV_W)))
    segs.append(('b_k', _seg('b_k', KV_W)))
    segs.append(('a_ik', _seg('a_ik', IDX_DIM, LANES)))
    n_rope = sum(len(s) for _, s in segs)
    segs.append(('a_v', _seg('a_v', KV_W)))
    segs.append(('a_vs', _swap_halves(_seg('a_v', KV_W))))
    segs.append(('b_v', _seg('b_v', KV_W)))
    segs.append(('b_vs', _swap_halves(_seg('b_v', KV_W))))
    segs.append(('c_qw', _wide_q_cols('c_q')))
    segs.append(('c_k', _seg('c_k', KV_W)))
    segs.append(('c_v', _seg('c_v', KV_W)))
    segs.append(('c_vs', _swap_halves(_seg('c_v', KV_W))))
    segs.append(('a_z', _seg('a_z', BRANCH_W)))
    segs.append(('b_z', _seg('b_z', BRANCH_W)))
    segs.append(('c_z', _seg('c_z', BRANCH_W)))
    misc = np.full((LANES,), -1, np.int64)
    misc[MISC_F:MISC_F + N_HEADS] = IN_OFF['c_f'] + np.arange(N_HEADS)
    misc[MISC_W:MISC_W + IDX_HEADS] = IN_OFF['a_iw'] + np.arange(IDX_HEADS)
    segs.append(('misc', misc))
    segs.append(('gates', _seg('gates', N_BRANCH * D_MODEL)))
    off = {}
    o = 0
    for n, s in segs:
        off[n] = (o, len(s))
        o += len(s)
    return np.concatenate([s for _, s in segs]), off, n_rope, o


W_SRC, W_OFF, W_ROPE, W_TOTAL = _build_layout()


def _dot(a, b):
    return jnp.dot(a, b, preferred_element_type=F32)


def _dot_nt(a, b):
    return lax.dot_general(a, b, (((1,), (1,)), ((), ())), preferred_element_type=F32)


def _split3(x):
    hi = x.astype(BF16)
    r1 = x - hi.astype(F32)
    mid = r1.astype(BF16)
    lo = (r1 - mid.astype(F32)).astype(BF16)
    return hi, mid, lo


def _dot3(x, m_bf16):
    hi, mid, lo = _split3(x)
    return _dot(hi, m_bf16) + _dot(mid, m_bf16) + _dot(lo, m_bf16)


def _dot3_left(m_bf16, x):
    hi, mid, lo = _split3(x)
    return _dot(m_bf16, hi) + _dot(m_bf16, mid) + _dot(m_bf16, lo)


def _proj_kernel(x_ref, g_ref, w_ref, cos_ref, sin_ref, bf_ref,
                 qa_ref, qb_ref, qi_ref, qc_ref, kva_ref, kvb_ref, kvc_ref, kidx_ref,
                 akv_ref, aidx_ref, bkv_ref, ckv_ref, logf_ref, misc_ref, sz_ref, gate_ref):
    x = x_ref[...]
    ms = jnp.mean(x * x, axis=-1, keepdims=True)
    h = (x * lax.rsqrt(ms + RMS_EPS) * g_ref[...]).astype(BF16)
    cos = cos_ref[...]
    sin = sin_ref[...]
    tm = x.shape[0]
    lane = lax.broadcasted_iota(jnp.int32, (tm, LANES), 1)
    first_half = (lane % HEAD_DIM) < (HEAD_DIM // 2)

    def proj(name):
        o, w = W_OFF[name]
        return _dot(h, w_ref[:, o:o + w])

    def rope(y):
        partner = jnp.where(first_half, pltpu.roll(y, LANES - HEAD_DIM // 2, 1),
                            pltpu.roll(y, HEAD_DIM // 2, 1))
        return y * cos + partner * sin

    for name, ref in (('a_qw', qa_ref), ('b_qw', qb_ref)):
        y = proj(name)
        for c in range(N_HEADS):
            ref[:, c * LANES:(c + 1) * LANES] = (
                rope(y[:, c * LANES:(c + 1) * LANES]) * ATTN_SCALE).astype(BF16)
    y = proj('a_iqw')
    for c in range(IDX_HEADS):
        qi_ref[:, c * LANES:(c + 1) * LANES] = rope(y[:, c * LANES:(c + 1) * LANES]).astype(BF16)
    y = proj('c_qw')
    qc_ref[...] = (y * ATTN_SCALE).astype(BF16)

    for kname, vname, f32_ref, bf_out in (('a_k', 'a_v', akv_ref, kva_ref),
                                          ('b_k', 'b_v', bkv_ref, kvb_ref),
                                          ('c_k', 'c_v', ckv_ref, kvc_ref)):
        k = proj(kname)
        if kname != 'c_k':
            k = rope(k)
        v = proj(vname)
        vs = proj(vname + 's')
        f32_ref[:, 0:LANES] = k
        f32_ref[:, LANES:2 * LANES] = v
        bf_out[:, 0:LANES] = k.astype(BF16)
        bf_out[:, LANES:2 * LANES] = v.astype(BF16)
        bf_out[:, 2 * LANES:3 * LANES] = vs.astype(BF16)

    ik = rope(proj('a_ik'))
    aidx_ref[...] = ik[:, 0:IDX_DIM]
    kidx_ref[...] = ik.astype(BF16)

    for i, name in enumerate(('a_z', 'b_z', 'c_z')):
        z = proj(name)
        sz_ref[:, i * BRANCH_W:(i + 1) * BRANCH_W] = (z * jax.nn.sigmoid(z)).astype(BF16)

    y = proj('misc')
    t = y + bf_ref[...]
    logf = jnp.minimum(t, 0.0) - jnp.log1p(jnp.exp(-jnp.abs(t)))
    misc = jnp.where(lane < MISC_W, logf, y * IDX_SCALE)
    misc_ref[...] = misc
    logf_ref[...] = misc[:, MISC_F:MISC_F + N_HEADS]

    y = proj('gates')
    gate_ref[...] = jax.nn.sigmoid(y).astype(BF16)


def _proj_call(x, g, w, cos, sin, bf_pad, tm):
    n = x.shape[0]
    row = lambda width: pl.BlockSpec((tm, width), lambda i: (i, 0))
    const = lambda shape: pl.BlockSpec(shape, lambda i: (0, 0))
    out_shapes = [
        ((n, N_HEADS * LANES), BF16),
        ((n, N_HEADS * LANES), BF16),
        ((n, IDX_HEADS * LANES), BF16),
        ((n, N_HEADS * LANES), BF16),
        ((n, 3 * LANES), BF16),
        ((n, 3 * LANES), BF16),
        ((n, 3 * LANES), BF16),
        ((n, LANES), BF16),
        ((n, 2 * KV_W), F32),
        ((n, IDX_DIM), F32),
        ((n, 2 * KV_W), F32),
        ((n, 2 * KV_W), F32),
        ((n, N_HEADS), F32),
        ((n, LANES), F32),
        ((n, N_BRANCH * BRANCH_W), BF16),
        ((n, N_BRANCH * D_MODEL), BF16),
    ]
    return pl.pallas_call(
        _proj_kernel,
        grid=(n // tm,),
        in_specs=[row(D_MODEL), const((1, D_MODEL)),
                  pl.BlockSpec((D_MODEL, W_TOTAL), lambda i: (0, 0), pipeline_mode=pl.Buffered(1)),
                  row(LANES), row(LANES), const((1, LANES))],
        out_specs=[row(s[1]) for s, _ in out_shapes],
        out_shape=[jax.ShapeDtypeStruct(s, d) for s, d in out_shapes],
        compiler_params=pltpu.CompilerParams(dimension_semantics=("arbitrary",),
                                             vmem_limit_bytes=VMEM_LIMIT),
        name="proj",
    )(x, g, w, cos, sin, bf_pad)


def _init_state(m_sc, l_sc, acc_sc):
    m_sc[...] = jnp.full(m_sc.shape, NEG_INF, F32)
    l_sc[...] = jnp.zeros(l_sc.shape, F32)
    acc_sc[...] = jnp.zeros(acc_sc.shape, F32)


def _online_update(h, s, vt, m_sc, l_sc, acc_sc):
    m_prev = m_sc[h]
    m_new = jnp.maximum(m_prev, jnp.max(s, axis=1, keepdims=True))
    m_safe = jnp.where(m_new == NEG_INF, 0.0, m_new)
    p = jnp.exp(s - m_safe)
    alpha = jnp.exp(m_prev - m_safe)
    l_sc[h] = alpha * l_sc[h] + jnp.sum(p, axis=1, keepdims=True)
    acc_sc[h] = alpha * acc_sc[h] + _dot(p.astype(BF16), vt)
    m_sc[h] = m_new


def _v_off(h):
    return LANES if (h // GROUP) == (h % 2) else 2 * LANES


def _write_pairs(o_ref, l_sc, acc_sc, rows):
    lane = lax.broadcasted_iota(jnp.int32, (rows, LANES), 1)
    for j in range(N_HEADS // 2):
        even = acc_sc[2 * j] / l_sc[2 * j]
        odd = acc_sc[2 * j + 1] / l_sc[2 * j + 1]
        o_ref[:, j * LANES:(j + 1) * LANES] = jnp.where(lane < HEAD_DIM, even, odd).astype(o_ref.dtype)


def _causal_bias(tq, tk):
    row = lax.broadcasted_iota(jnp.int32, (tq, tk), 0)
    col = lax.broadcasted_iota(jnp.int32, (tq, tk), 1)
    return jnp.where(col <= row, 0.0, NEG_INF).astype(F32)


def _fox_aug_mats():
    pk = np.zeros((3, LANES, LANES), np.float32)
    pq = np.zeros((3, LANES, N_HEADS * LANES), np.float32)
    kconst = np.zeros((1, LANES), np.float32)
    qconst = np.zeros((1, N_HEADS * LANES), np.float32)
    for t in range(3):
        kconst[0, AUG_ONE + t] = 1.0
        for h in range(N_HEADS):
            pk[t, MISC_F + h, AUG_C + 3 * h + t] = -1.0
            pq[t, MISC_F + h, h * LANES + AUG_ONE + t] = 1.0
            qconst[0, h * LANES + AUG_C + 3 * h + t] = 1.0
    return pk, pq, kconst, qconst


def _foxprep_kernel(misc_ref, kv_ref, ltri_ref, pk_ref, pq_ref, kconst_ref, qconst_ref,
                    kaug_ref, qaug_ref, carry_sc):
    @pl.when(pl.program_id(1) == 0)
    def _():
        carry_sc[...] = jnp.zeros(carry_sc.shape, F32)

    lf = misc_ref[...]
    rows = lf.shape[0]
    c = _dot3_left(ltri_ref[...], lf) + carry_sc[0:1, :]
    carry_sc[0:1, :] = c[rows - 1:rows, :]
    ka = kconst_ref[...]
    qa = qconst_ref[...]
    for t, part in enumerate(_split3(c)):
        ka = ka + _dot(part, pk_ref[t])
        qa = qa + _dot(part, pq_ref[t])
    kaug_ref[:, 0:LANES] = kv_ref[...]
    kaug_ref[:, LANES:2 * LANES] = ka.astype(BF16)
    qaug_ref[...] = qa.astype(BF16)


def _foxprep_call(misc, kvc, batch, t_len, chunk):
    n = misc.shape[0]
    nch = t_len // chunk
    pk, pq, kconst, qconst = _fox_aug_mats()
    ltri = np.tril(np.ones((chunk, chunk), np.float32))
    rowblk = lambda width, col=0: pl.BlockSpec((chunk, width), lambda b, i: (b * nch + i, col))
    const = lambda shape: pl.BlockSpec(shape, lambda b, i: tuple(0 for _ in shape))
    return pl.pallas_call(
        _foxprep_kernel,
        grid=(batch, nch),
        in_specs=[rowblk(LANES), rowblk(LANES), const((chunk, chunk)), const((3, LANES, LANES)),
                  const((3, LANES, N_HEADS * LANES)), const((1, LANES)), const((1, N_HEADS * LANES))],
        out_specs=[rowblk(2 * LANES), rowblk(N_HEADS * LANES)],
        out_shape=[jax.ShapeDtypeStruct((n, 2 * LANES), BF16),
                   jax.ShapeDtypeStruct((n, N_HEADS * LANES), BF16)],
        scratch_shapes=[pltpu.VMEM((SUBLANES, LANES), F32)],
        compiler_params=pltpu.CompilerParams(dimension_semantics=("arbitrary", "arbitrary"),
                                             vmem_limit_bytes=VMEM_LIMIT),
        name="foxprep",
    )(misc, kvc, jnp.asarray(ltri, BF16), jnp.asarray(pk, BF16), jnp.asarray(pq, BF16),
      jnp.asarray(kconst, F32), jnp.asarray(qconst, F32))


def _fox_prompt_kernel(qw_ref, qaug_ref, kaug_ref, kv_ref, o_ref, qp_sc, m_sc, l_sc, acc_sc, *, tq, tk):
    qi = pl.program_id(1)
    _init_state(m_sc, l_sc, acc_sc)
    for h in range(N_HEADS):
        qp_sc[h, :, 0:LANES] = qw_ref[:, h * LANES:(h + 1) * LANES]
        qp_sc[h, :, LANES:2 * LANES] = qaug_ref[:, h * LANES:(h + 1) * LANES]

    def tile(ki, bias):
        kt = kaug_ref[pl.ds(pl.multiple_of(ki * tk, tk), tk), :]
        for h in range(N_HEADS):
            s = _dot_nt(qp_sc[h], kt)
            if bias is not None:
                s = s + bias
            vt = kv_ref[pl.ds(pl.multiple_of(ki * tk, tk), tk), _v_off(h):_v_off(h) + LANES]
            _online_update(h, s, vt, m_sc, l_sc, acc_sc)

    def body(ki, c):
        tile(ki, None)
        return c

    lax.fori_loop(0, qi, body, 0)
    tile(qi, _causal_bias(tq, tk))
    _write_pairs(o_ref, l_sc, acc_sc, tq)


def _fox_prompt_call(qw, qaug, kaug, kvc, batch, t_len, tq):
    n = qw.shape[0]
    nq = t_len // tq
    return pl.pallas_call(
        functools.partial(_fox_prompt_kernel, tq=tq, tk=tq),
        grid=(batch, nq),
        in_specs=[pl.BlockSpec((tq, N_HEADS * LANES), lambda b, i: (b * nq + i, 0)),
                  pl.BlockSpec((tq, N_HEADS * LANES), lambda b, i: (b * nq + i, 0)),
                  pl.BlockSpec((t_len, 2 * LANES), lambda b, i: (b, 0)),
                  pl.BlockSpec((t_len, 3 * LANES), lambda b, i: (b, 0))],
        out_specs=pl.BlockSpec((tq, BRANCH_W), lambda b, i: (b * nq + i, 0)),
        out_shape=jax.ShapeDtypeStruct((n, BRANCH_W), BF16),
        scratch_shapes=[pltpu.VMEM((N_HEADS, tq, 2 * LANES), BF16),
                        pltpu.VMEM((N_HEADS, tq, 1), F32), pltpu.VMEM((N_HEADS, tq, 1), F32),
                        pltpu.VMEM((N_HEADS, tq, LANES), F32)],
        compiler_params=pltpu.CompilerParams(dimension_semantics=("arbitrary", "arbitrary"),
                                             vmem_limit_bytes=VMEM_LIMIT),
        name="fox_prompt",
    )(qw, qaug, kaug, kvc)


def _sortable_key(score):
    bits = lax.bitcast_convert_type(score + 0.0, jnp.int32)
    return jnp.where(bits < 0, bits ^ jnp.int32(0x7FFFFFFF), bits)


def _selected(key, idx, thr, cutoff):
    sel = jnp.logical_or(key > thr, jnp.logical_and(key == thr, idx <= cutoff))
    return jnp.logical_and(sel, key != INT_MIN)


def _lane_column(x, j):
    lane = lax.broadcasted_iota(jnp.int32, x.shape, 1)
    return jnp.sum(jnp.where(lane == j, x, 0.0), axis=1, keepdims=True)


def _kth_largest(count_ge, rows, k):
    zero = jnp.zeros((rows, 1), jnp.int32)
    ans = jnp.where(count_ge(zero) >= k, 0, INT_MIN).astype(jnp.int32)

    def body(i, ans):
        cand = ans + lax.shift_left(jnp.int32(1), jnp.int32(30) - i)
        return jnp.where(count_ge(cand) >= k, cand, ans)

    return lax.fori_loop(0, 31, body, ans)


def _tie_cutoff(count_tie_lt, need, rows, nbits):
    def body(i, x):
        cand = x + lax.shift_left(jnp.int32(1), jnp.int32(nbits - 1) - i)
        return jnp.where(count_tie_lt(cand) < need, cand, x)

    return lax.fori_loop(0, nbits, body, jnp.zeros((rows, 1), jnp.int32))


def _dsa_prompt_kernel(qw_ref, qi_ref, misc_ref, kidx_ref, kv_ref, o_ref,
                       s_sc, m_sc, l_sc, acc_sc, *, tq, tk, n_sel, nbits):
    qi = pl.program_id(1)
    nk = qi + 1
    row = lax.broadcasted_iota(jnp.int32, (tq, tk), 0)
    col = lax.broadcasted_iota(jnp.int32, (tq, tk), 1)
    misc = misc_ref[...]
    wcols = [_lane_column(misc, MISC_W + h) for h in range(IDX_HEADS)]

    def score_tile(ki):
        kt = kidx_ref[pl.ds(pl.multiple_of(ki * tk, tk), tk), :]
        acc = jnp.zeros((tq, tk), F32)
        for h in range(IDX_HEADS):
            r = _dot_nt(qi_ref[:, h * LANES:(h + 1) * LANES], kt)
            acc = acc + jnp.maximum(r, 0.0) * wcols[h]
        return _sortable_key(acc)

    def fill(ki, c):
        s_sc[ki] = score_tile(ki)
        return c

    lax.fori_loop(0, qi, fill, 0)
    s_sc[qi] = jnp.where(col <= row, score_tile(qi), INT_MIN)

    def fold(x):
        acc = x[:, 0:LANES]
        for c in range(1, tk // LANES):
            acc = acc + x[:, c * LANES:(c + 1) * LANES]
        return acc

    def counter(pred):
        def count(arg):
            def body(ki, acc):
                return acc + fold(jnp.where(pred(s_sc[ki], ki, arg), 1.0, 0.0))
            acc = lax.fori_loop(0, nk, body, jnp.zeros((tq, LANES), F32))
            return jnp.sum(acc, axis=1, keepdims=True)
        return count

    kf = float(n_sel)
    thr = _kth_largest(counter(lambda key, ki, x: key >= x), tq, kf)
    cnt_gt = counter(lambda key, ki, x: key > x)(thr)
    cnt_ge = counter(lambda key, ki, x: key >= x)(thr)
    need = kf - cnt_gt
    tie_lt = counter(lambda key, ki, x: jnp.logical_and(key == thr, (ki * tk + col) < x))
    cutoff = lax.cond(jnp.max(cnt_ge) > kf,
                      lambda: _tie_cutoff(tie_lt, need, tq, nbits),
                      lambda: jnp.full((tq, 1), 2 ** 30, jnp.int32))

    def to_bias(ki, c):
        sel = _selected(s_sc[ki], ki * tk + col, thr, cutoff)
        s_sc[ki] = lax.bitcast_convert_type(jnp.where(sel, 0.0, NEG_INF).astype(F32), jnp.int32)
        return c

    lax.fori_loop(0, nk, to_bias, 0)

    _init_state(m_sc, l_sc, acc_sc)

    def attend(ki, c):
        bias = lax.bitcast_convert_type(s_sc[ki], F32)
        kt = kv_ref[pl.ds(pl.multiple_of(ki * tk, tk), tk), 0:LANES]
        for h in range(N_HEADS):
            s = _dot_nt(qw_ref[:, h * LANES:(h + 1) * LANES], kt) + bias
            vt = kv_ref[pl.ds(pl.multiple_of(ki * tk, tk), tk), _v_off(h):_v_off(h) + LANES]
            _online_update(h, s, vt, m_sc, l_sc, acc_sc)
        return c

    lax.fori_loop(0, nk, attend, 0)
    _write_pairs(o_ref, l_sc, acc_sc, tq)


def _dsa_prompt_call(qw, qi, misc, kidx, kva, batch, t_len, tq):
    n = qw.shape[0]
    nq = t_len // tq
    n_sel = min(DSA_TOPK, t_len // 4)
    nbits = max(1, (t_len - 1).bit_length())
    rowblk = lambda width: pl.BlockSpec((tq, width), lambda b, i: (b * nq + i, 0))
    seq = lambda width: pl.BlockSpec((t_len, width), lambda b, i: (b, 0))
    return pl.pallas_call(
        functools.partial(_dsa_prompt_kernel, tq=tq, tk=tq, n_sel=n_sel, nbits=nbits),
        grid=(batch, nq),
        in_specs=[rowblk(N_HEADS * LANES), rowblk(IDX_HEADS * LANES), rowblk(LANES),
                  seq(LANES), seq(3 * LANES)],
        out_specs=rowblk(BRANCH_W),
        out_shape=jax.ShapeDtypeStruct((n, BRANCH_W), BF16),
        scratch_shapes=[pltpu.VMEM((nq, tq, tq), jnp.int32),
                        pltpu.VMEM((N_HEADS, tq, 1), F32), pltpu.VMEM((N_HEADS, tq, 1), F32),
                        pltpu.VMEM((N_HEADS, tq, LANES), F32)],
        compiler_params=pltpu.CompilerParams(dimension_semantics=("arbitrary", "arbitrary"),
                                             vmem_limit_bytes=VMEM_LIMIT),
        name="dsa_prompt",
    )(qw, qi, misc, kidx, kva)


def _kmean_kernel(k_ref, o_ref):
    o_ref[...] = jnp.mean(k_ref[...], axis=0, keepdims=True)


def _kmean_call(kv_f32, n_blocks):
    return pl.pallas_call(
        _kmean_kernel,
        grid=(n_blocks,),
        in_specs=[pl.BlockSpec((MOBA_BLOCK, LANES), lambda i: (i, 0))],
        out_specs=pl.BlockSpec((None, 1, LANES), lambda i: (i, 0, 0)),
        out_shape=jax.ShapeDtypeStruct((n_blocks, 1, LANES), F32),
        name="moba_kmean",
    )(kv_f32)


def _top_blocks_bias(gate, n_valid_lt, n_sel):
    lane_i = lax.broadcasted_iota(jnp.int32, gate.shape, 1)
    lane = lane_i.astype(F32)
    cur = jnp.where(lane_i < n_valid_lt, gate, NEG_INF)
    bias = jnp.full(gate.shape, NEG_INF, F32)
    for _ in range(n_sel):
        mx = jnp.max(cur, axis=1, keepdims=True)
        first = jnp.min(jnp.where(cur == mx, lane, float(LANES)), axis=1, keepdims=True)
        pick = jnp.logical_and(mx > NEG_INF, lane == first)
        bias = jnp.where(pick, 0.0, bias)
        cur = jnp.where(pick, NEG_INF, cur)
    return bias


def _moba_prompt_kernel(qw_ref, kmean_ref, kv_ref, o_ref, selb_sc, m_sc, l_sc, acc_sc, *, tq, n_sel):
    qi = pl.program_id(1)
    tk = tq
    lane = lax.broadcasted_iota(jnp.int32, (tq, LANES), 1)
    km = kmean_ref[...]
    for h in range(N_HEADS):
        gate = _dot_nt(qw_ref[:, h * LANES:(h + 1) * LANES], km)
        selb_sc[h] = _top_blocks_bias(gate, qi, n_sel)
    _init_state(m_sc, l_sc, acc_sc)

    def tile(ki, causal):
        kt = kv_ref[pl.ds(pl.multiple_of(ki * tk, tk), tk), 0:LANES]
        for h in range(N_HEADS):
            s = _dot_nt(qw_ref[:, h * LANES:(h + 1) * LANES], kt)
            if causal is not None:
                s = s + causal
            else:
                s = s + jnp.max(jnp.where(lane == ki, selb_sc[h], NEG_INF), axis=1, keepdims=True)
            vt = kv_ref[pl.ds(pl.multiple_of(ki * tk, tk), tk), _v_off(h):_v_off(h) + LANES]
            _online_update(h, s, vt, m_sc, l_sc, acc_sc)

    tile(qi, _causal_bias(tq, tk))

    def body(ki, c):
        tile(ki, None)
        return c

    lax.fori_loop(0, qi, body, 0)
    _write_pairs(o_ref, l_sc, acc_sc, tq)


def _moba_prompt_call(qw, kmean, kvb, batch, t_len):
    n = qw.shape[0]
    tq = MOBA_BLOCK
    nq = t_len // tq
    n_sel = min(MOBA_TOPK, nq)
    rowblk = lambda width: pl.BlockSpec((tq, width), lambda b, i: (b * nq + i, 0))
    return pl.pallas_call(
        functools.partial(_moba_prompt_kernel, tq=tq, n_sel=n_sel),
        grid=(batch, nq),
        in_specs=[rowblk(N_HEADS * LANES),
                  pl.BlockSpec((None, LANES, LANES), lambda b, i: (b, 0, 0)),
                  pl.BlockSpec((t_len, 3 * LANES), lambda b, i: (b, 0))],
        out_specs=rowblk(BRANCH_W),
        out_shape=jax.ShapeDtypeStruct((n, BRANCH_W), BF16),
        scratch_shapes=[pltpu.VMEM((N_HEADS, tq, LANES), F32),
                        pltpu.VMEM((N_HEADS, tq, 1), F32), pltpu.VMEM((N_HEADS, tq, 1), F32),
                        pltpu.VMEM((N_HEADS, tq, LANES), F32)],
        compiler_params=pltpu.CompilerParams(dimension_semantics=("arbitrary", "arbitrary"),
                                             vmem_limit_bytes=VMEM_LIMIT),
        name="moba_prompt",
    )(qw, kmean, kvb)


def _merge_kernel(x_ref, oa_ref, ob_ref, oc_ref, sz_ref, gate_ref, wbr_ref, wout_ref, gfin_ref,
                  o_ref, *, final):
    m = None
    for n, o_n in enumerate((oa_ref, ob_ref, oc_ref)):
        a = o_n[...] * sz_ref[:, n * BRANCH_W:(n + 1) * BRANCH_W]
        u = _dot(a, wbr_ref[n])
        t = gate_ref[:, n * D_MODEL:(n + 1) * D_MODEL].astype(F32) * u
        m = t if m is None else m + t
    y = x_ref[...] + _dot(m.astype(BF16), wout_ref[...])
    if final:
        ms = jnp.mean(y * y, axis=-1, keepdims=True)
        y = y * lax.rsqrt(ms + RMS_EPS) * gfin_ref[...]
    o_ref[...] = y


def _merge_call(x, oa, ob, oc, sz, gates, wbr, wout, gfin, tm, final):
    n = x.shape[0]
    row = lambda width: pl.BlockSpec((tm, width), lambda i: (i, 0))
    return pl.pallas_call(
        functools.partial(_merge_kernel, final=final),
        grid=(n // tm,),
        in_specs=[row(D_MODEL), row(BRANCH_W), row(BRANCH_W), row(BRANCH_W),
                  row(N_BRANCH * BRANCH_W), row(N_BRANCH * D_MODEL),
                  pl.BlockSpec((N_BRANCH, BRANCH_W, D_MODEL), lambda i: (0, 0, 0)),
                  pl.BlockSpec((D_MODEL, D_MODEL), lambda i: (0, 0)),
                  pl.BlockSpec((1, D_MODEL), lambda i: (0, 0))],
        out_specs=row(D_MODEL),
        out_shape=jax.ShapeDtypeStruct((n, D_MODEL), F32),
        compiler_params=pltpu.CompilerParams(dimension_semantics=("arbitrary",),
                                             vmem_limit_bytes=VMEM_LIMIT),
        name="merge",
    )(x, oa, ob, oc, sz, gates, wbr, wout, gfin)


def _expand8(x):
    return jnp.concatenate([jnp.broadcast_to(x[h:h + 1, :], (SUBLANES, x.shape[1]))
                            for h in range(x.shape[0])], axis=0)


def _tile8(x, reps):
    return jnp.concatenate([x] * reps, axis=0)


def _new_key_bias(rows, n_new):
    t8 = lax.broadcasted_iota(jnp.int32, (rows, LANES), 0) % SUBLANES
    lane = lax.broadcasted_iota(jnp.int32, (rows, LANES), 1)
    return jnp.where(lane <= jnp.minimum(t8, n_new - 1), 0.0, NEG_INF).astype(F32)


def _finish_decode(o_ref, l_sc, acc_sc):
    o_ref[...] = acc_sc[0] / l_sc[0]


def _dsa_decode_kernel(pt_ref, qi_ref, w_ref, q_ref, knew_idx_ref, knew_ref, vnew_ref, *rest,
                       n_pg, n_groups, n_sel, n_new, nbits):
    idx_pages = rest[:n_pg]
    kv_pages = rest[n_pg:2 * n_pg]
    o_ref, s_sc, thr_sc, cut_sc, m_sc, l_sc, acc_sc = rest[2 * n_pg:]
    ph = pl.program_id(1)
    g = pl.program_id(2)
    width = n_pg * LANES
    rows_i = IDX_HEADS * SUBLANES
    rows_q = N_HEADS * SUBLANES

    def scores(q_bf16, keys_bf16):
        r = _dot_nt(q_bf16, keys_bf16)
        r = jnp.maximum(r, 0.0) * w_ref[...]
        acc = r[0:SUBLANES]
        for h in range(1, IDX_HEADS):
            acc = acc + r[h * SUBLANES:(h + 1) * SUBLANES]
        return _sortable_key(acc)

    @pl.when(ph == 0)
    def _():
        for j in range(n_pg):
            s_sc[g, :, j * LANES:(j + 1) * LANES] = scores(qi_ref[:, 0:IDX_DIM],
                                                           idx_pages[j][...].astype(BF16))

    @pl.when(jnp.logical_and(ph == 0, g == n_groups - 1))
    def _():
        key = scores(qi_ref[...], knew_idx_ref[...])
        visible = _new_key_bias(SUBLANES, n_new) == 0.0
        t8 = lax.broadcasted_iota(jnp.int32, (SUBLANES, LANES), 0)
        key = jnp.where(jnp.logical_and(visible, t8 < n_new), key, INT_MIN)
        s_sc[n_groups] = jnp.full((SUBLANES, width), INT_MIN, jnp.int32)
        s_sc[n_groups, :, 0:LANES] = key
        lane_w = lax.broadcasted_iota(jnp.int32, (SUBLANES, width), 1)

        def counter(pred):
            def count(arg):
                acc = jnp.zeros((SUBLANES, width), F32)
                for gi in range(n_groups + 1):
                    acc = acc + jnp.where(pred(s_sc[gi], gi * width + lane_w, arg), 1.0, 0.0)
                return jnp.sum(acc, axis=1, keepdims=True)
            return count

        kf = float(n_sel)
        thr = _kth_largest(counter(lambda key, idx, x: key >= x), SUBLANES, kf)
        need = kf - counter(lambda key, idx, x: key > x)(thr)
        tie_lt = counter(lambda key, idx, x: jnp.logical_and(key == thr, idx < x))
        cut = _tie_cutoff(tie_lt, need, SUBLANES, nbits)
        thr_sc[...] = jnp.broadcast_to(thr, (SUBLANES, LANES))
        cut_sc[...] = jnp.broadcast_to(cut, (SUBLANES, LANES))
        _init_state(m_sc, l_sc, acc_sc)

    def sel_bias(key, idx):
        sel = _selected(key, idx, thr_sc[...], cut_sc[...])
        return _tile8(jnp.where(sel, 0.0, NEG_INF).astype(F32), N_HEADS)

    @pl.when(ph == 1)
    def _():
        lane = lax.broadcasted_iota(jnp.int32, (SUBLANES, LANES), 1)
        for j in range(n_pg):
            page = kv_pages[j][...]
            kt = page[:, 0:LANES].astype(BF16)
            vt = page[:, LANES:2 * LANES].astype(BF16)
            key = s_sc[g, :, j * LANES:(j + 1) * LANES]
            s = _dot_nt(q_ref[...], kt) + sel_bias(key, g * width + j * LANES + lane)
            _online_update(0, s, vt, m_sc, l_sc, acc_sc)

    @pl.when(jnp.logical_and(ph == 1, g == n_groups - 1))
    def _():
        lane = lax.broadcasted_iota(jnp.int32, (SUBLANES, LANES), 1)
        key = s_sc[n_groups, :, 0:LANES]
        s = _dot_nt(q_ref[...], knew_ref[...]) + sel_bias(key, n_groups * width + lane)
        _online_update(0, s, vnew_ref[...], m_sc, l_sc, acc_sc)
        _finish_decode(o_ref, l_sc, acc_sc)


def _page_specs(n_pg, block, index_of):
    specs = []
    for j in range(n_pg):
        specs.append(pl.BlockSpec(block, functools.partial(index_of, j)))
    return specs


def _dsa_decode_call(page_table, layer_off, qi_rows, w_rows, q_rows, knew_idx, knew, vnew,
                     cache_idx, cache_kv, n_pg):
    nb, n_pages = page_table.shape
    n_groups = n_pages // n_pg
    past = n_pages * LANES
    n_new = 4
    total = past + n_new
    n_sel = min(DSA_TOPK, total // 4)
    nbits = max(1, ((n_groups + 1) * n_pg * LANES - 1).bit_length())

    def idx_map(j, b, ph, g, pt):
        gg = jnp.where(ph == 0, g, n_groups - 1)
        return (pt[b, gg * n_pg + j] + layer_off, 0, 0)

    def kv_map(j, b, ph, g, pt):
        gg = jnp.where(ph == 1, g, 0)
        return (pt[b, gg * n_pg + j] + layer_off, 0, 0)

    per_b = lambda r, c: pl.BlockSpec((None, r, c), lambda b, ph, g, pt: (b, 0, 0))
    rows_q = N_HEADS * SUBLANES
    grid_spec = pltpu.PrefetchScalarGridSpec(
        num_scalar_prefetch=1,
        grid=(nb, 2, n_groups),
        in_specs=[per_b(IDX_HEADS * SUBLANES, LANES), per_b(IDX_HEADS * SUBLANES, LANES),
                  per_b(rows_q, LANES), per_b(LANES, LANES), per_b(LANES, LANES), per_b(LANES, LANES)]
        + _page_specs(n_pg, (None, LANES, IDX_DIM), idx_map)
        + _page_specs(n_pg, (None, LANES, 2 * KV_W), kv_map),
        out_specs=per_b(rows_q, LANES),
        scratch_shapes=[pltpu.VMEM((n_groups + 1, SUBLANES, n_pg * LANES), jnp.int32),
                        pltpu.VMEM((SUBLANES, LANES), jnp.int32), pltpu.VMEM((SUBLANES, LANES), jnp.int32),
                        pltpu.VMEM((1, rows_q, 1), F32), pltpu.VMEM((1, rows_q, 1), F32),
                        pltpu.VMEM((1, rows_q, LANES), F32)],
    )
    return pl.pallas_call(
        functools.partial(_dsa_decode_kernel, n_pg=n_pg, n_groups=n_groups, n_sel=n_sel,
                          n_new=n_new, nbits=nbits),
        grid_spec=grid_spec,
        out_shape=jax.ShapeDtypeStruct((nb, rows_q, LANES), F32),
        compiler_params=pltpu.CompilerParams(
            dimension_semantics=("arbitrary", "arbitrary", "arbitrary"), vmem_limit_bytes=VMEM_LIMIT),
        name="dsa_decode",
    )(page_table, qi_rows, w_rows, q_rows, knew_idx, knew, vnew,
      *([cache_idx] * n_pg), *([cache_kv] * n_pg))


def _moba_decode_kernel(pt_ref, q_ref, knew_ref, vnew_ref, *rest, n_pg, n_groups, n_sel, n_new):
    kv_pages = rest[:n_pg]
    o_ref, gate_sc, mb_sc, lb_sc, accb_sc, m_sc, l_sc, acc_sc = rest[n_pg:]
    g = pl.program_id(1)
    pages_per_block = MOBA_BLOCK // LANES
    blocks_per_step = n_pg // pages_per_block
    n_blocks = n_groups * blocks_per_step
    rows = N_HEADS * SUBLANES
    q = q_ref[...]
    qf = q.astype(F32)

    for jb in range(blocks_per_step):
        blk = jnp.concatenate([kv_pages[jb * pages_per_block + p][...] for p in range(pages_per_block)],
                              axis=0)
        kf = blk[:, 0:LANES]
        kmean = jnp.mean(kf, axis=0, keepdims=True).astype(BF16).astype(F32)
        gate = jnp.sum(qf * kmean, axis=1, keepdims=True)
        s = _dot_nt(q, kf.astype(BF16))
        mj = jnp.max(s, axis=1, keepdims=True)
        p = jnp.exp(s - mj)
        b = g * blocks_per_step + jb
        gate_sc[b] = jnp.broadcast_to(gate, (rows, LANES))
        mb_sc[b] = jnp.broadcast_to(mj, (rows, LANES))
        lb_sc[b] = jnp.broadcast_to(jnp.sum(p, axis=1, keepdims=True), (rows, LANES))
        accb_sc[b] = _dot(p.astype(BF16), blk[:, LANES:2 * LANES].astype(BF16))

    @pl.when(g == n_groups - 1)
    def _():
        _init_state(m_sc, l_sc, acc_sc)
        s = _dot_nt(q, knew_ref[...]) + _new_key_bias(rows, n_new)
        _online_update(0, s, vnew_ref[...], m_sc, l_sc, acc_sc)
        m_own = jnp.broadcast_to(m_sc[0], (rows, LANES))

        def pick_round(_, sel_any):
            def mx_body(b, mx):
                return jnp.maximum(mx, gate_sc[b])
            mx = lax.fori_loop(0, n_blocks, mx_body, jnp.full((rows, LANES), NEG_INF, F32))

            def first_body(b, first):
                return jnp.minimum(first, jnp.where(gate_sc[b] == mx, b, n_blocks))
            first = lax.fori_loop(0, n_blocks, first_body, jnp.full((rows, LANES), n_blocks, jnp.int32))

            def mark_body(b, c):
                hit = jnp.logical_and(first == b, mx > NEG_INF)
                gate_sc[b] = jnp.where(hit, NEG_INF, gate_sc[b])
                lb_sc[b] = jnp.where(hit, -lb_sc[b], lb_sc[b])
                return c
            lax.fori_loop(0, n_blocks, mark_body, 0)
            return sel_any

        lax.fori_loop(0, n_sel, pick_round, 0)

        def max_body(b, mt):
            return jnp.maximum(mt, jnp.where(lb_sc[b] < 0.0, mb_sc[b], NEG_INF))
        m_tot = lax.fori_loop(0, n_blocks, max_body, m_own)

        def comb_body(b, carry):
            l_tot, acc_tot = carry
            picked = lb_sc[b] < 0.0
            wgt = jnp.where(picked, jnp.exp(mb_sc[b] - m_tot), 0.0)
            return l_tot - wgt * lb_sc[b], acc_tot + wgt * accb_sc[b]

        w_own = jnp.exp(m_own - m_tot)
        l0 = w_own * jnp.broadcast_to(l_sc[0], (rows, LANES))
        a0 = w_own * acc_sc[0]
        l_tot, acc_tot = lax.fori_loop(0, n_blocks, comb_body, (l0, a0))
        o_ref[...] = acc_tot / l_tot


def _moba_decode_call(page_table, layer_off, q_rows, knew, vnew, cache_kv, n_pg):
    nb, n_pages = page_table.shape
    n_groups = n_pages // n_pg
    n_blocks = n_pages * LANES // MOBA_BLOCK
    n_sel = min(MOBA_TOPK, n_blocks + 1)
    rows_q = N_HEADS * SUBLANES

    def kv_map(j, b, g, pt):
        return (pt[b, g * n_pg + j] + layer_off, 0, 0)

    per_b = lambda r, c: pl.BlockSpec((None, r, c), lambda b, g, pt: (b, 0, 0))
    slab = lambda: pltpu.VMEM((n_blocks, rows_q, LANES), F32)
    grid_spec = pltpu.PrefetchScalarGridSpec(
        num_scalar_prefetch=1,
        grid=(nb, n_groups),
        in_specs=[per_b(rows_q, LANES), per_b(LANES, LANES), per_b(LANES, LANES)]
        + _page_specs(n_pg, (None, LANES, 2 * KV_W), kv_map),
        out_specs=per_b(rows_q, LANES),
        scratch_shapes=[slab(), slab(), slab(), slab(),
                        pltpu.VMEM((1, rows_q, 1), F32), pltpu.VMEM((1, rows_q, 1), F32),
                        pltpu.VMEM((1, rows_q, LANES), F32)],
    )
    return pl.pallas_call(
        functools.partial(_moba_decode_kernel, n_pg=n_pg, n_groups=n_groups, n_sel=n_sel, n_new=4),
        grid_spec=grid_spec,
        out_shape=jax.ShapeDtypeStruct((nb, rows_q, LANES), F32),
        compiler_params=pltpu.CompilerParams(dimension_semantics=("arbitrary", "arbitrary"),
                                             vmem_limit_bytes=VMEM_LIMIT),
        name="moba_decode",
    )(page_table, q_rows, knew, vnew, *([cache_kv] * n_pg))


def _fox_decode_kernel(pt_ref, q_ref, knew_ref, vnew_ref, lfnew_rows_ref, lfnew_t_ref,
                       tri_ref, upper_ref, sfx_ref, *rest, n_pg, n_groups, n_new):
    kv_pages = rest[:n_pg]
    lf_pages = rest[n_pg:2 * n_pg]
    o_ref, cq_sc, carry_sc, m_sc, l_sc, acc_sc = rest[2 * n_pg:]
    g = pl.program_id(1)
    rows = N_HEADS * SUBLANES
    q = q_ref[...]

    @pl.when(g == 0)
    def _():
        cq_rows = _dot3_left(tri_ref[...], lfnew_rows_ref[...])
        cq_sc[...] = cq_rows
        new_cum_t = _dot3(lfnew_t_ref[...], upper_ref[...])
        carry_sc[...] = jnp.zeros(carry_sc.shape, F32)
        _init_state(m_sc, l_sc, acc_sc)
        s = (_dot_nt(q, knew_ref[...]) + cq_rows - _expand8(new_cum_t)
             + _new_key_bias(rows, n_new))
        _online_update(0, s, vnew_ref[...], m_sc, l_sc, acc_sc)

    for j in range(n_pg - 1, -1, -1):
        page = kv_pages[j][...]
        lft = lf_pages[j][...]
        suffix = _dot3(lft, sfx_ref[...]) + carry_sc[:, 0:1]
        carry_sc[...] = carry_sc[...] + jnp.sum(lft, axis=1, keepdims=True)
        s = _dot_nt(q, page[:, 0:LANES].astype(BF16)) + cq_sc[...] + _expand8(suffix)
        _online_update(0, s, page[:, LANES:2 * LANES].astype(BF16), m_sc, l_sc, acc_sc)

    @pl.when(g == n_groups - 1)
    def _():
        _finish_decode(o_ref, l_sc, acc_sc)


def _fox_decode_call(page_table, layer_off, q_rows, knew, vnew, lfnew_rows, lfnew_t,
                     cache_kv, cache_lft, n_pg):
    nb, n_pages = page_table.shape
    n_groups = n_pages // n_pg
    rows_q = N_HEADS * SUBLANES
    r = np.arange(rows_q)
    tri = ((r[:, None] // SUBLANES == r[None, :] // SUBLANES)
           & (r[None, :] % SUBLANES <= r[:, None] % SUBLANES)).astype(np.float32)
    lane = np.arange(LANES)
    upper = (lane[:, None] <= lane[None, :]).astype(np.float32)
    sfx = (lane[:, None] > lane[None, :]).astype(np.float32)

    def kv_map(j, b, g, pt):
        return (pt[b, (n_groups - 1 - g) * n_pg + j] + layer_off, 0, 0)

    per_b = lambda rr, c: pl.BlockSpec((None, rr, c), lambda b, g, pt: (b, 0, 0))
    const = lambda rr, c: pl.BlockSpec((rr, c), lambda b, g, pt: (0, 0))
    grid_spec = pltpu.PrefetchScalarGridSpec(
        num_scalar_prefetch=1,
        grid=(nb, n_groups),
        in_specs=[per_b(rows_q, LANES), per_b(LANES, LANES), per_b(LANES, LANES),
                  per_b(rows_q, LANES), per_b(SUBLANES, LANES),
                  const(rows_q, rows_q), const(LANES, LANES), const(LANES, LANES)]
        + _page_specs(n_pg, (None, LANES, 2 * KV_W), kv_map)
        + _page_specs(n_pg, (None, SUBLANES, LANES), kv_map),
        out_specs=per_b(rows_q, LANES),
        scratch_shapes=[pltpu.VMEM((rows_q, LANES), F32), pltpu.VMEM((SUBLANES, LANES), F32),
                        pltpu.VMEM((1, rows_q, 1), F32), pltpu.VMEM((1, rows_q, 1), F32),
                        pltpu.VMEM((1, rows_q, LANES), F32)],
    )
    return pl.pallas_call(
        functools.partial(_fox_decode_kernel, n_pg=n_pg, n_groups=n_groups, n_new=4),
        grid_spec=grid_spec,
        out_shape=jax.ShapeDtypeStruct((nb, rows_q, LANES), F32),
        compiler_params=pltpu.CompilerParams(dimension_semantics=("arbitrary", "arbitrary"),
                                             vmem_limit_bytes=VMEM_LIMIT),
        name="fox_decode",
    )(page_table, q_rows, knew, vnew, lfnew_rows, lfnew_t,
      jnp.asarray(tri, BF16), jnp.asarray(upper, BF16), jnp.asarray(sfx, BF16),
      *([cache_kv] * n_pg), *([cache_lft] * n_pg))


def _rope_tables(pos):
    half = HEAD_DIM // 2
    freqs = ROPE_THETA ** (-jnp.arange(half, dtype=F32) / half)
    ang = pos.astype(F32)[:, None] * freqs[None, :]
    cos = jnp.tile(jnp.cos(ang), (1, LANES // half))
    sin = jnp.sin(ang)
    sin = jnp.tile(jnp.concatenate([-sin, sin], axis=1), (1, LANES // HEAD_DIM))
    return cos, sin


def _layout_w_in(w_in_l):
    cols = jnp.asarray(np.maximum(W_SRC, 0), jnp.int32)
    keep = jnp.asarray(W_SRC >= 0)
    return jnp.where(keep[None, :], jnp.take(w_in_l, cols, axis=1), 0.0).astype(BF16)


def _decode_rows(a, nb, t_new, heads):
    a = a.reshape(nb, t_new, heads, LANES).transpose(0, 2, 1, 3)
    a = jnp.pad(a, ((0, 0), (0, 0), (0, SUBLANES - t_new), (0, 0)))
    return a.reshape(nb, heads * SUBLANES, LANES)


def _new_rows(a, nb, t_new):
    a = a.reshape(nb, t_new, LANES)
    return jnp.pad(a, ((0, 0), (0, LANES - t_new), (0, 0)))


def _pick_pages_per_step(n_pages):
    for c in (16, 8, 4, 2):
        if n_pages % c == 0:
            return c
    raise ValueError("page count must be even")


def _prompt_branches(proj, batch, t_len):
    (qa, qb, qi, qc, kva, kvb, kvc, kidx, _, _, bkv, _, _, misc, _, _) = proj
    n_blocks = t_len // MOBA_BLOCK
    o_a = _dsa_prompt_call(qa, qi, misc, kidx, kva, batch, t_len, MOBA_BLOCK)
    kmean = _kmean_call(bkv, batch * n_blocks).reshape(batch, n_blocks, LANES)
    kmean = jnp.pad(kmean, ((0, 0), (0, LANES - n_blocks), (0, 0))).astype(BF16)
    o_b = _moba_prompt_call(qb, kmean, kvb, batch, t_len)
    kaug, qaug = _foxprep_call(misc, kvc, batch, t_len, LANES)
    o_c = _fox_prompt_call(qc, qaug, kaug, kvc, batch, t_len, MOBA_BLOCK)
    return o_a, o_b, o_c


def _sample_branches(proj, nb, t_new, page_table, layer_off, caches):
    (qa, qb, qi, qc, kva, kvb, kvc, kidx, _, _, _, _, logf, misc, _, _) = proj
    ca_idx, ca_kv, cb_kv, cc_kv, cc_lft = caches
    n_pg = _pick_pages_per_step(page_table.shape[1])

    def head_rows(col, heads):
        a = col.reshape(nb, t_new, heads).transpose(0, 2, 1)
        a = jnp.pad(a, ((0, 0), (0, 0), (0, SUBLANES - t_new)))
        return jnp.broadcast_to(a.reshape(nb, heads * SUBLANES, 1), (nb, heads * SUBLANES, LANES))

    def new_kv(kv):
        return _new_rows(kv[:, 0:LANES], nb, t_new), _new_rows(kv[:, LANES:2 * LANES], nb, t_new)

    d_a = _dsa_decode_call(
        page_table, layer_off, _decode_rows(qi, nb, t_new, IDX_HEADS),
        head_rows(misc[:, MISC_W:MISC_W + IDX_HEADS], IDX_HEADS),
        _decode_rows(qa, nb, t_new, N_HEADS), _new_rows(kidx, nb, t_new), *new_kv(kva),
        ca_idx, ca_kv, n_pg)
    d_b = _moba_decode_call(page_table, layer_off, _decode_rows(qb, nb, t_new, N_HEADS),
                            *new_kv(kvb), cb_kv, n_pg)
    lf_t = jnp.pad(logf.reshape(nb, t_new, N_HEADS).transpose(0, 2, 1),
                   ((0, 0), (0, 0), (0, LANES - t_new)))
    d_c = _fox_decode_call(page_table, layer_off, _decode_rows(qc, nb, t_new, N_HEADS),
                           *new_kv(kvc), head_rows(logf, N_HEADS), lf_t, cc_kv, cc_lft, n_pg)

    def compact(d):
        d = d.reshape(nb, N_KV_HEADS, GROUP, SUBLANES, N_KV_HEADS, HEAD_DIM)[:, :, :, :t_new]
        d = jnp.stack([d[:, gidx, :, :, gidx, :] for gidx in range(N_KV_HEADS)], axis=1)
        return d.transpose(0, 3, 1, 2, 4).reshape(nb * t_new, BRANCH_W).astype(BF16)

    return compact(d_a), compact(d_b), compact(d_c)


def kernel(x_prompt, x_sample, cache_a_kv, cache_a_idx, cache_b_kv, cache_c_kv, cache_c_logf,
           page_table, norm_g, w_in, b_forget, w_branch, w_out, final_norm_g):
    batch, t_len, _ = x_prompt.shape
    nb, t_new, _ = x_sample.shape
    depth = norm_g.shape[0]
    n_pool, page_size = cache_a_kv.shape[1], cache_a_kv.shape[2]
    n_pages = page_table.shape[1]
    past_len = n_pages * page_size
    assert page_size == LANES and t_len % MOBA_BLOCK == 0 and past_len % MOBA_BLOCK == 0 and t_new == 4
    n_p = batch * t_len
    n_s = nb * t_new

    pos_p = jnp.tile(jnp.arange(t_len, dtype=jnp.int32), batch)
    pos_s = jnp.tile(past_len + jnp.arange(t_new, dtype=jnp.int32), nb)
    cos_p, sin_p = _rope_tables(pos_p)
    cos_s, sin_s = _rope_tables(pos_s)

    kv_pages = lambda c: c.reshape(depth * n_pool, page_size, 2 * KV_W)
    ca_kv, cb_kv, cc_kv = kv_pages(cache_a_kv), kv_pages(cache_b_kv), kv_pages(cache_c_kv)
    ca_idx = cache_a_idx.reshape(depth * n_pool, page_size, IDX_DIM)
    cc_lft = cache_c_logf.reshape(depth * n_pool, page_size, N_HEADS).transpose(0, 2, 1)

    xp = x_prompt.reshape(n_p, D_MODEL)
    xs = x_sample.reshape(n_s, D_MODEL)
    gfin = final_norm_g.reshape(1, D_MODEL)
    rows_p = []
    rows_s = []
    for l in range(depth):
        w = _layout_w_in(w_in[l])
        g = norm_g[l].reshape(1, D_MODEL)
        bf_pad = jnp.zeros((1, LANES), F32).at[0, MISC_F:MISC_F + N_HEADS].set(b_forget[l])
        wbr = w_branch[l].astype(BF16)
        wout = w_out[l].astype(BF16)
        final = l == depth - 1

        proj_p = _proj_call(xp, g, w, cos_p, sin_p, bf_pad, 256)
        rows_p.append(proj_p[8:13])
        o_a, o_b, o_c = _prompt_branches(proj_p, batch, t_len)
        xp = _merge_call(xp, o_a, o_b, o_c, proj_p[14], proj_p[15], wbr, wout, gfin, 256, final)

        proj_s = _proj_call(xs, g, w, cos_s, sin_s, bf_pad, n_s)
        rows_s.append(proj_s[8:13])
        o_a, o_b, o_c = _sample_branches(proj_s, nb, t_new, page_table, l * n_pool,
                                         (ca_idx, ca_kv, cb_kv, cc_kv, cc_lft))
        xs = _merge_call(xs, o_a, o_b, o_c, proj_s[14], proj_s[15], wbr, wout, gfin, n_s, final)

    def stack(rows, i, b, t, tail):
        return jnp.stack([r[i] for r in rows]).reshape((depth, b, t) + tail)

    kv_tail = (2, N_KV_HEADS, HEAD_DIM)
    outs_p = (stack(rows_p, 0, batch, t_len, kv_tail), stack(rows_p, 1, batch, t_len, (IDX_DIM,)),
              stack(rows_p, 2, batch, t_len, kv_tail), stack(rows_p, 3, batch, t_len, kv_tail),
              stack(rows_p, 4, batch, t_len, (N_HEADS,)))
    outs_s = (stack(rows_s, 0, nb, t_new, kv_tail), stack(rows_s, 1, nb, t_new, (IDX_DIM,)),
              stack(rows_s, 2, nb, t_new, kv_tail), stack(rows_s, 3, nb, t_new, kv_tail),
              stack(rows_s, 4, nb, t_new, (N_HEADS,)))
    return (xp.reshape(batch, t_len, D_MODEL), xs.reshape(nb, t_new, D_MODEL)) + outs_p + outs_s
```

```python
import functools

import numpy as np
import jax
import jax.numpy as jnp
from jax import lax
from jax.experimental import pallas as pl
from jax.experimental.pallas import tpu as pltpu

D_MODEL = 1024
HEAD_DIM = 64
N_HEADS = 8
N_KV_HEADS = 2
GROUP = N_HEADS // N_KV_HEADS
BRANCH_W = N_HEADS * HEAD_DIM
KV_W = N_KV_HEADS * HEAD_DIM
N_BRANCH = 3
IDX_HEADS = 4
IDX_DIM = 64
DSA_TOPK = 256
MOBA_BLOCK = 256
MOBA_TOPK = 3
ROPE_THETA = 10000.0
RMS_EPS = 1e-6
ATTN_SCALE = HEAD_DIM ** -0.5
IDX_SCALE = (IDX_HEADS ** -0.5) * (IDX_DIM ** -0.5)

LANES = 128
SUBLANES = 8
VMEM_LIMIT = 56 * 1024 * 1024
INT_MIN = -2 ** 31
NEG_INF = float("-inf")
F32 = jnp.float32
BF16 = jnp.bfloat16

VT_ROWS = 80
VT_ONE = HEAD_DIM

IN_SPLITS = (
    ('a_q', BRANCH_W), ('a_k', KV_W), ('a_v', KV_W), ('a_z', BRANCH_W),
    ('a_iq', IDX_HEADS * IDX_DIM), ('a_ik', IDX_DIM), ('a_iw', IDX_HEADS),
    ('b_q', BRANCH_W), ('b_k', KV_W), ('b_v', KV_W), ('b_z', BRANCH_W),
    ('c_q', BRANCH_W), ('c_k', KV_W), ('c_v', KV_W), ('c_z', BRANCH_W), ('c_f', N_HEADS),
    ('gates', N_BRANCH * D_MODEL),
)
IN_OFF = {}
_o = 0
for _n, _w in IN_SPLITS:
    IN_OFF[_n] = _o
    _o += _w
IN_WIDTH = _o

MISC_F = 0
MISC_W = 8
AUG_ONE = 0
AUG_C = 3


def _wide_q_cols(name):
    src = np.full((N_HEADS * LANES,), -1, np.int64)
    for h in range(N_HEADS):
        g = h // GROUP
        dst = h * LANES + g * HEAD_DIM
        src[dst:dst + HEAD_DIM] = IN_OFF[name] + h * HEAD_DIM + np.arange(HEAD_DIM)
    return src


def _seg(name, width, pad_to=None):
    src = IN_OFF[name] + np.arange(width)
    if pad_to is not None and pad_to > width:
        src = np.concatenate([src, np.full((pad_to - width,), -1, np.int64)])
    return src


def _build_layout():
    segs = []
    segs.append(('a_qw', _wide_q_cols('a_q')))
    segs.append(('b_qw', _wide_q_cols('b_q')))
    iq = np.full((IDX_HEADS * LANES,), -1, np.int64)
    for h in range(IDX_HEADS):
        iq[h * LANES:h * LANES + IDX_DIM] = IN_OFF['a_iq'] + h * IDX_DIM + np.arange(IDX_DIM)
    segs.append(('a_iqw', iq))
    segs.append(('a_k', _seg('a_k', KV_W)))
    segs.append(('b_k', _seg('b_k', KV_W)))
    segs.append(('a_ik', _seg('a_ik', IDX_DIM, LANES)))
    segs.append(('a_v', _seg('a_v', KV_W)))
    segs.append(('b_v', _seg('b_v', KV_W)))
    segs.append(('c_qw', _wide_q_cols('c_q')))
    segs.append(('c_k', _seg('c_k', KV_W)))
    segs.append(('c_v', _seg('c_v', KV_W)))
    segs.append(('a_z', _seg('a_z', BRANCH_W)))
    segs.append(('b_z', _seg('b_z', BRANCH_W)))
    segs.append(('c_z', _seg('c_z', BRANCH_W)))
    misc = np.full((LANES,), -1, np.int64)
    misc[MISC_F:MISC_F + N_HEADS] = IN_OFF['c_f'] + np.arange(N_HEADS)
    misc[MISC_W:MISC_W + IDX_HEADS] = IN_OFF['a_iw'] + np.arange(IDX_HEADS)
    segs.append(('misc', misc))
    segs.append(('gates', _seg('gates', N_BRANCH * D_MODEL)))
    off = {}
    o = 0
    for n, s in segs:
        off[n] = (o, len(s))
        o += len(s)
    return np.concatenate([s for _, s in segs]), off, o


W_SRC, W_OFF, W_TOTAL = _build_layout()


def _dot(a, b):
    return jnp.dot(a, b, preferred_element_type=F32)


def _dot_nt(a, b):
    return lax.dot_general(a, b, (((1,), (1,)), ((), ())), preferred_element_type=F32)


def _split3(x):
    hi = x.astype(BF16)
    r1 = x - hi.astype(F32)
    mid = r1.astype(BF16)
    lo = (r1 - mid.astype(F32)).astype(BF16)
    return hi, mid, lo


def _dot3(x, m_bf16):
    hi, mid, lo = _split3(x)
    return _dot(hi, m_bf16) + _dot(mid, m_bf16) + _dot(lo, m_bf16)


def _dot3_left(m_bf16, x):
    hi, mid, lo = _split3(x)
    return _dot(m_bf16, hi) + _dot(m_bf16, mid) + _dot(m_bf16, lo)


def _proj_kernel(x_ref, g_ref, w_ref, cos_ref, sin_ref, bf_ref,
                 qa_ref, qb_ref, qi_ref, qc_ref, kva_ref, kvb_ref, kvc_ref, kidx_ref,
                 akv_ref, aidx_ref, bkv_ref, ckv_ref, logf_ref, misc_ref, sz_ref, gate_ref):
    x = x_ref[...]
    ms = jnp.mean(x * x, axis=-1, keepdims=True)
    h = (x * lax.rsqrt(ms + RMS_EPS) * g_ref[...]).astype(BF16)
    cos = cos_ref[...]
    sin = sin_ref[...]
    tm = x.shape[0]
    lane = lax.broadcasted_iota(jnp.int32, (tm, LANES), 1)
    first_half = (lane % HEAD_DIM) < (HEAD_DIM // 2)

    def proj(name):
        o, w = W_OFF[name]
        return _dot(h, w_ref[:, o:o + w])

    def rope(y):
        partner = jnp.where(first_half, pltpu.roll(y, LANES - HEAD_DIM // 2, 1),
                            pltpu.roll(y, HEAD_DIM // 2, 1))
        return y * cos + partner * sin

    for name, ref in (('a_qw', qa_ref), ('b_qw', qb_ref)):
        y = proj(name)
        for c in range(N_HEADS):
            ref[:, c * LANES:(c + 1) * LANES] = (
                rope(y[:, c * LANES:(c + 1) * LANES]) * ATTN_SCALE).astype(BF16)
    y = proj('a_iqw')
    for c in range(IDX_HEADS):
        qi_ref[:, c * LANES:(c + 1) * LANES] = rope(y[:, c * LANES:(c + 1) * LANES]).astype(BF16)
    y = proj('c_qw')
    qc_ref[...] = (y * ATTN_SCALE).astype(BF16)

    for kname, vname, f32_ref, bf_out in (('a_k', 'a_v', akv_ref, kva_ref),
                                          ('b_k', 'b_v', bkv_ref, kvb_ref),
                                          ('c_k', 'c_v', ckv_ref, kvc_ref)):
        k = proj(kname)
        if kname != 'c_k':
            k = rope(k)
        v = proj(vname)
        f32_ref[:, 0:LANES] = k
        f32_ref[:, LANES:2 * LANES] = v
        bf_out[:, 0:LANES] = k.astype(BF16)
        bf_out[:, LANES:2 * LANES] = v.astype(BF16)

    ik = rope(proj('a_ik'))
    aidx_ref[...] = ik[:, 0:IDX_DIM]
    kidx_ref[...] = ik.astype(BF16)

    for i, name in enumerate(('a_z', 'b_z', 'c_z')):
        z = proj(name)
        sz_ref[:, i * BRANCH_W:(i + 1) * BRANCH_W] = (z * jax.nn.sigmoid(z)).astype(BF16)

    y = proj('misc')
    t = y + bf_ref[...]
    logf = jnp.minimum(t, 0.0) - jnp.log1p(jnp.exp(-jnp.abs(t)))
    misc = jnp.where(lane < MISC_W, logf, y * IDX_SCALE)
    misc_ref[...] = misc
    logf_ref[...] = misc[:, MISC_F:MISC_F + N_HEADS]

    y = proj('gates')
    gate_ref[...] = jax.nn.sigmoid(y).astype(BF16)


def _proj_call(x, g, w, cos, sin, bf_pad, tm):
    n = x.shape[0]
    row = lambda width: pl.BlockSpec((tm, width), lambda i: (i, 0))
    const = lambda shape: pl.BlockSpec(shape, lambda i: (0, 0))
    out_shapes = [
        ((n, N_HEADS * LANES), BF16),
        ((n, N_HEADS * LANES), BF16),
        ((n, IDX_HEADS * LANES), BF16),
        ((n, N_HEADS * LANES), BF16),
        ((n, 2 * KV_W), BF16),
        ((n, 2 * KV_W), BF16),
        ((n, 2 * KV_W), BF16),
        ((n, LANES), BF16),
        ((n, 2 * KV_W), F32),
        ((n, IDX_DIM), F32),
        ((n, 2 * KV_W), F32),
        ((n, 2 * KV_W), F32),
        ((n, N_HEADS), F32),
        ((n, LANES), F32),
        ((n, N_BRANCH * BRANCH_W), BF16),
        ((n, N_BRANCH * D_MODEL), BF16),
    ]
    return pl.pallas_call(
        _proj_kernel,
        grid=(n // tm,),
        in_specs=[row(D_MODEL), const((1, D_MODEL)),
                  pl.BlockSpec((D_MODEL, W_TOTAL), lambda i: (0, 0), pipeline_mode=pl.Buffered(1)),
                  row(LANES), row(LANES), const((1, LANES))],
        out_specs=[row(s[1]) for s, _ in out_shapes],
        out_shape=[jax.ShapeDtypeStruct(s, d) for s, d in out_shapes],
        compiler_params=pltpu.CompilerParams(dimension_semantics=("arbitrary",),
                                             vmem_limit_bytes=VMEM_LIMIT),
        name="proj",
    )(x, g, w, cos, sin, bf_pad)


def _rows8(x):
    return x.reshape(x.shape[0] // SUBLANES, SUBLANES, x.shape[1])


def _allreduce8(x8, op):
    for shift in (4, 2, 1):
        x8 = op(x8, pltpu.roll(x8, shift, 0))
    return x8


def _init_state_t(m_sc, acc_sc):
    m_sc[...] = jnp.full(m_sc.shape, NEG_INF, F32)
    acc_sc[...] = jnp.zeros(acc_sc.shape, F32)


def _update_t(h, s, vt, m_sc, acc_sc):
    s3 = _rows8(s)
    m_prev = m_sc[h]
    m_new = jnp.maximum(m_prev, _allreduce8(jnp.max(s3, axis=0), jnp.maximum))
    m_safe = jnp.where(m_new == NEG_INF, 0.0, m_new)
    p = jnp.exp(s3 - m_safe[None]).reshape(s.shape).astype(BF16)
    alpha = jnp.exp(m_prev - m_safe)
    acc = acc_sc[h]
    acc_sc[h] = (_rows8(acc) * alpha[None]).reshape(acc.shape) + _dot(vt, p)
    m_sc[h] = m_new


def _group_queries(qg_sc, head_block, tq):
    for h in range(N_HEADS):
        g, j = divmod(h, GROUP)
        qg_sc[g, :, j * tq:(j + 1) * tq] = head_block(h)


def _per_head(x):
    return jnp.concatenate([x] * GROUP, axis=1)


def _write_t(o_ref, acc_sc, tq):
    for h in range(N_HEADS):
        g, j = divmod(h, GROUP)
        acc = acc_sc[g, :, j * tq:(j + 1) * tq]
        o_ref[h * HEAD_DIM:(h + 1) * HEAD_DIM, :] = (
            acc[0:HEAD_DIM] / acc[VT_ONE:VT_ONE + 1]).astype(o_ref.dtype)


def _causal_bias_t(tk, tq):
    key = lax.broadcasted_iota(jnp.int32, (tk, tq), 0)
    qry = lax.broadcasted_iota(jnp.int32, (tk, tq), 1)
    return jnp.where(key <= qry, 0.0, NEG_INF).astype(F32)


def _attn_scratch(tq, dk):
    return [pltpu.VMEM((N_KV_HEADS, dk, GROUP * tq), BF16),
            pltpu.VMEM((N_KV_HEADS, SUBLANES, GROUP * tq), F32),
            pltpu.VMEM((N_KV_HEADS, VT_ROWS, GROUP * tq), F32)]


def _prompt_params():
    return pltpu.CompilerParams(dimension_semantics=("arbitrary", "arbitrary"),
                                vmem_limit_bytes=VMEM_LIMIT)


def _fox_aug_mats():
    pk = np.zeros((3, LANES, LANES), np.float32)
    pq = np.zeros((3, LANES, N_HEADS * LANES), np.float32)
    kconst = np.zeros((1, LANES), np.float32)
    qconst = np.zeros((1, N_HEADS * LANES), np.float32)
    for t in range(3):
        kconst[0, AUG_ONE + t] = 1.0
        for h in range(N_HEADS):
            pk[t, MISC_F + h, AUG_C + 3 * h + t] = -1.0
            pq[t, MISC_F + h, h * LANES + AUG_ONE + t] = 1.0
            qconst[0, h * LANES + AUG_C + 3 * h + t] = 1.0
    return pk, pq, kconst, qconst


def _foxprep_kernel(misc_ref, kv_ref, ltri_ref, pk_ref, pq_ref, kconst_ref, qconst_ref,
                    kaug_ref, qaug_ref, carry_sc):
    @pl.when(pl.program_id(1) == 0)
    def _():
        carry_sc[...] = jnp.zeros(carry_sc.shape, F32)

    lf = misc_ref[...]
    rows = lf.shape[0]
    c = _dot3_left(ltri_ref[...], lf) + carry_sc[0:1, :]
    carry_sc[0:1, :] = c[rows - 1:rows, :]
    ka = kconst_ref[...]
    qa = qconst_ref[...]
    for t, part in enumerate(_split3(c)):
        ka = ka + _dot(part, pk_ref[t])
        qa = qa + _dot(part, pq_ref[t])
    kaug_ref[:, 0:LANES] = kv_ref[...]
    kaug_ref[:, LANES:2 * LANES] = ka.astype(BF16)
    qaug_ref[...] = qa.astype(BF16)


def _foxprep_call(misc, kvc, batch, t_len, chunk):
    n = misc.shape[0]
    nch = t_len // chunk
    pk, pq, kconst, qconst = _fox_aug_mats()
    ltri = np.tril(np.ones((chunk, chunk), np.float32))
    rowblk = lambda width, col=0: pl.BlockSpec((chunk, width), lambda b, i: (b * nch + i, col))
    const = lambda shape: pl.BlockSpec(shape, lambda b, i: tuple(0 for _ in shape))
    return pl.pallas_call(
        _foxprep_kernel,
        grid=(batch, nch),
        in_specs=[rowblk(LANES), rowblk(LANES), const((chunk, chunk)), const((3, LANES, LANES)),
                  const((3, LANES, N_HEADS * LANES)), const((1, LANES)), const((1, N_HEADS * LANES))],
        out_specs=[rowblk(2 * LANES), rowblk(N_HEADS * LANES)],
        out_shape=[jax.ShapeDtypeStruct((n, 2 * LANES), BF16),
                   jax.ShapeDtypeStruct((n, N_HEADS * LANES), BF16)],
        scratch_shapes=[pltpu.VMEM((SUBLANES, LANES), F32)],
        compiler_params=_prompt_params(),
        name="foxprep",
    )(misc, kvc, jnp.asarray(ltri, BF16), jnp.asarray(pk, BF16), jnp.asarray(pq, BF16),
      jnp.asarray(kconst, F32), jnp.asarray(qconst, F32))


def _fox_prompt_kernel(qt_ref, kaug_ref, vt_ref, o_ref, qg_sc, m_sc, acc_sc, *, tq, tk):
    qi = pl.program_id(1)
    _init_state_t(m_sc, acc_sc)
    _group_queries(qg_sc, lambda h: qt_ref[h], tq)

    def tile(ki, bias):
        kt = kaug_ref[pl.ds(pl.multiple_of(ki * tk, tk), tk), :]
        for g in range(N_KV_HEADS):
            s = _dot(kt, qg_sc[g])
            if bias is not None:
                s = s + bias
            _update_t(g, s, vt_ref[ki, g], m_sc, acc_sc)

    def body(ki, c):
        tile(ki, None)
        return c

    lax.fori_loop(0, qi, body, 0)
    tile(qi, _per_head(_causal_bias_t(tk, tq)))
    _write_t(o_ref, acc_sc, tq)


def _vt_spec(nk, tk):
    return pl.BlockSpec((nk, N_KV_HEADS, VT_ROWS, tk), lambda b, i: (b, 0, 0, 0))


def _fox_prompt_call(qpt, kaug, vt, batch, t_len, tq):
    n = kaug.shape[0]
    nq = t_len // tq
    return pl.pallas_call(
        functools.partial(_fox_prompt_kernel, tq=tq, tk=tq),
        grid=(batch, nq),
        in_specs=[pl.BlockSpec((N_HEADS, 2 * LANES, tq), lambda b, i: (0, 0, b * nq + i)),
                  pl.BlockSpec((t_len, 2 * LANES), lambda b, i: (b, 0)),
                  _vt_spec(nq, tq)],
        out_specs=pl.BlockSpec((BRANCH_W, tq), lambda b, i: (0, b * nq + i)),
        out_shape=jax.ShapeDtypeStruct((BRANCH_W, n), BF16),
        scratch_shapes=_attn_scratch(tq, 2 * LANES),
        compiler_params=_prompt_params(),
        name="fox_prompt",
    )(qpt, kaug, vt)


def _sortable_key(score):
    bits = lax.bitcast_convert_type(score + 0.0, jnp.int32)
    return jnp.where(bits < 0, bits ^ jnp.int32(0x7FFFFFFF), bits)


def _selected(key, idx, thr, cutoff):
    sel = jnp.logical_or(key > thr, jnp.logical_and(key == thr, idx <= cutoff))
    return jnp.logical_and(sel, key != INT_MIN)


def _kth_largest(count_ge, shape, k):
    zero = jnp.zeros(shape, jnp.int32)
    ans = jnp.where(count_ge(zero) >= k, 0, INT_MIN).astype(jnp.int32)

    def body(i, ans):
        cand = ans + lax.shift_left(jnp.int32(1), jnp.int32(30) - i)
        return jnp.where(count_ge(cand) >= k, cand, ans)

    return lax.fori_loop(0, 31, body, ans)


def _tie_cutoff(count_tie_lt, need, shape, nbits):
    def body(i, x):
        cand = x + lax.shift_left(jnp.int32(1), jnp.int32(nbits - 1) - i)
        return jnp.where(count_tie_lt(cand) < need, cand, x)

    return lax.fori_loop(0, nbits, body, jnp.zeros(shape, jnp.int32))


def _dsa_prompt_kernel(qt_ref, qit_ref, misct_ref, kidx_ref, k_ref, vt_ref, o_ref,
                       s_sc, qig_sc, qg_sc, m_sc, acc_sc, *, tq, tk, n_sel, nbits):
    qi = pl.program_id(1)
    nk = qi + 1
    key_row = lax.broadcasted_iota(jnp.int32, (tk, tq), 0)
    qry_col = lax.broadcasted_iota(jnp.int32, (tk, tq), 1)
    row3 = _rows8(key_row)
    for h in range(IDX_HEADS):
        qig_sc[:, h * tq:(h + 1) * tq] = qit_ref[h * LANES:(h + 1) * LANES, :]
    wrow = jnp.concatenate([misct_ref[MISC_W + h:MISC_W + h + 1, :] for h in range(IDX_HEADS)], axis=1)

    def score_tile(ki):
        kt = kidx_ref[pl.ds(pl.multiple_of(ki * tk, tk), tk), :]
        r = jnp.maximum(_dot(kt, qig_sc[...]), 0.0) * wrow
        acc = r[:, 0:tq]
        for h in range(1, IDX_HEADS):
            acc = acc + r[:, h * tq:(h + 1) * tq]
        return _sortable_key(acc)

    def fill(ki, c):
        s_sc[ki] = score_tile(ki)
        return c

    lax.fori_loop(0, qi, fill, 0)
    s_sc[qi] = jnp.where(key_row <= qry_col, score_tile(qi), INT_MIN)

    def counter(pred):
        def count(arg):
            def body(ki, acc):
                hit = pred(_rows8(s_sc[ki]), ki * tk + row3, arg[None])
                return acc + jnp.sum(jnp.where(hit, 1.0, 0.0), axis=0)
            acc = lax.fori_loop(0, nk, body, jnp.zeros((SUBLANES, tq), F32))
            return _allreduce8(acc, jnp.add)
        return count

    kf = float(n_sel)
    stat = (SUBLANES, tq)
    thr = _kth_largest(counter(lambda key, idx, x: key >= x), stat, kf)
    cnt_gt = counter(lambda key, idx, x: key > x)(thr)
    cnt_ge = counter(lambda key, idx, x: key >= x)(thr)
    need = kf - cnt_gt
    tie_lt = counter(lambda key, idx, x: jnp.logical_and(key == thr[None], idx < x))
    cutoff = lax.cond(jnp.max(cnt_ge) > kf,
                      lambda: _tie_cutoff(tie_lt, need, stat, nbits),
                      lambda: jnp.full(stat, 2 ** 30, jnp.int32))

    def to_bias(ki, c):
        sel = _selected(_rows8(s_sc[ki]), ki * tk + row3, thr[None], cutoff[None])
        bias = jnp.where(sel, 0.0, NEG_INF).astype(F32).reshape(tk, tq)
        s_sc[ki] = lax.bitcast_convert_type(bias, jnp.int32)
        return c

    lax.fori_loop(0, nk, to_bias, 0)

    _init_state_t(m_sc, acc_sc)
    _group_queries(qg_sc, lambda h: qt_ref[h * LANES:(h + 1) * LANES, :], tq)

    def attend(ki, c):
        bias = _per_head(lax.bitcast_convert_type(s_sc[ki], F32))
        kt = k_ref[pl.ds(pl.multiple_of(ki * tk, tk), tk), :]
        for g in range(N_KV_HEADS):
            _update_t(g, _dot(kt, qg_sc[g]) + bias, vt_ref[ki, g], m_sc, acc_sc)
        return c

    lax.fori_loop(0, nk, attend, 0)
    _write_t(o_ref, acc_sc, tq)


def _dsa_prompt_call(qat, qit, misct, kidx, kva, vt, batch, t_len, tq):
    n = kidx.shape[0]
    nq = t_len // tq
    n_sel = min(DSA_TOPK, t_len // 4)
    nbits = max(1, (t_len - 1).bit_length())
    colblk = lambda rows: pl.BlockSpec((rows, tq), lambda b, i: (0, b * nq + i))
    seq = lambda width: pl.BlockSpec((t_len, width), lambda b, i: (b, 0))
    return pl.pallas_call(
        functools.partial(_dsa_prompt_kernel, tq=tq, tk=tq, n_sel=n_sel, nbits=nbits),
        grid=(batch, nq),
        in_specs=[colblk(N_HEADS * LANES), colblk(IDX_HEADS * LANES), colblk(LANES),
                  seq(LANES), seq(LANES), _vt_spec(nq, tq)],
        out_specs=colblk(BRANCH_W),
        out_shape=jax.ShapeDtypeStruct((BRANCH_W, n), BF16),
        scratch_shapes=[pltpu.VMEM((nq, tq, tq), jnp.int32),
                        pltpu.VMEM((LANES, IDX_HEADS * tq), BF16)] + _attn_scratch(tq, LANES),
        compiler_params=_prompt_params(),
        name="dsa_prompt",
    )(qat, qit, misct, kidx, kva, vt)


def _kmean_kernel(k_ref, o_ref):
    o_ref[...] = jnp.mean(k_ref[...], axis=0, keepdims=True)


def _kmean_call(kv_f32, n_blocks):
    return pl.pallas_call(
        _kmean_kernel,
        grid=(n_blocks,),
        in_specs=[pl.BlockSpec((MOBA_BLOCK, LANES), lambda i: (i, 0))],
        out_specs=pl.BlockSpec((None, 1, LANES), lambda i: (i, 0, 0)),
        out_shape=jax.ShapeDtypeStruct((n_blocks, 1, LANES), F32),
        name="moba_kmean",
    )(kv_f32)


def _colmax(x):
    return _allreduce8(jnp.max(_rows8(x), axis=0), jnp.maximum)


def _top_blocks_bias_t(gate, n_valid_lt, n_sel):
    blk_i = lax.broadcasted_iota(jnp.int32, gate.shape, 0)
    blk = blk_i.astype(F32)
    cur3 = _rows8(jnp.where(blk_i < n_valid_lt, gate, NEG_INF))
    blk3 = _rows8(blk)
    bias3 = jnp.full(cur3.shape, NEG_INF, F32)
    for _ in range(n_sel):
        mx = _allreduce8(jnp.max(cur3, axis=0), jnp.maximum)
        first = _allreduce8(jnp.min(jnp.where(cur3 == mx[None], blk3, float(LANES)), axis=0),
                            jnp.minimum)
        pick = jnp.logical_and(mx[None] > NEG_INF, blk3 == first[None])
        bias3 = jnp.where(pick, 0.0, bias3)
        cur3 = jnp.where(pick, NEG_INF, cur3)
    return bias3.reshape(gate.shape)


def _moba_prompt_kernel(qt_ref, kmean_ref, k_ref, vt_ref, o_ref, selb_sc, qg_sc, m_sc, acc_sc,
                        *, tq, n_sel):
    qi = pl.program_id(1)
    tk = tq
    width = GROUP * tq
    blk = lax.broadcasted_iota(jnp.int32, (LANES, width), 0)
    km = kmean_ref[...]
    _group_queries(qg_sc, lambda h: qt_ref[h * LANES:(h + 1) * LANES, :], tq)
    for g in range(N_KV_HEADS):
        selb_sc[g] = _top_blocks_bias_t(_dot(km, qg_sc[g]), qi, n_sel)
    _init_state_t(m_sc, acc_sc)

    def tile(ki, causal):
        kt = k_ref[pl.ds(pl.multiple_of(ki * tk, tk), tk), :]
        for g in range(N_KV_HEADS):
            s = _dot(kt, qg_sc[g])
            if causal is not None:
                s = s + causal
            else:
                picked = _colmax(jnp.where(blk == ki, selb_sc[g], NEG_INF))
                s = (_rows8(s) + picked[None]).reshape(tk, width)
            _update_t(g, s, vt_ref[ki, g], m_sc, acc_sc)

    tile(qi, _per_head(_causal_bias_t(tk, tq)))

    def body(ki, c):
        tile(ki, None)
        return c

    lax.fori_loop(0, qi, body, 0)
    _write_t(o_ref, acc_sc, tq)


def _moba_prompt_call(qbt, kmean, kvb, vt, batch, t_len):
    n = kvb.shape[0]
    tq = MOBA_BLOCK
    nq = t_len // tq
    n_sel = min(MOBA_TOPK, nq)
    colblk = lambda rows: pl.BlockSpec((rows, tq), lambda b, i: (0, b * nq + i))
    return pl.pallas_call(
        functools.partial(_moba_prompt_kernel, tq=tq, n_sel=n_sel),
        grid=(batch, nq),
        in_specs=[colblk(N_HEADS * LANES),
                  pl.BlockSpec((None, LANES, LANES), lambda b, i: (b, 0, 0)),
                  pl.BlockSpec((t_len, LANES), lambda b, i: (b, 0)),
                  _vt_spec(nq, tq)],
        out_specs=colblk(BRANCH_W),
        out_shape=jax.ShapeDtypeStruct((BRANCH_W, n), BF16),
        scratch_shapes=[pltpu.VMEM((N_KV_HEADS, LANES, GROUP * tq), F32)] + _attn_scratch(tq, LANES),
        compiler_params=_prompt_params(),
        name="moba_prompt",
    )(qbt, kmean, kvb, vt)


def _merge_kernel(x_ref, oa_ref, ob_ref, oc_ref, sz_ref, gate_ref, wbr_ref, wout_ref, gfin_ref,
                  o_ref, *, final):
    m = None
    for n, o_n in enumerate((oa_ref, ob_ref, oc_ref)):
        a = o_n[...] * sz_ref[:, n * BRANCH_W:(n + 1) * BRANCH_W]
        u = _dot(a, wbr_ref[n])
        t = gate_ref[:, n * D_MODEL:(n + 1) * D_MODEL].astype(F32) * u
        m = t if m is None else m + t
    y = x_ref[...] + _dot(m.astype(BF16), wout_ref[...])
    if final:
        ms = jnp.mean(y * y, axis=-1, keepdims=True)
        y = y * lax.rsqrt(ms + RMS_EPS) * gfin_ref[...]
    o_ref[...] = y


def _merge_call(x, oa, ob, oc, sz, gates, wbr, wout, gfin, tm, final):
    n = x.shape[0]
    row = lambda width: pl.BlockSpec((tm, width), lambda i: (i, 0))
    return pl.pallas_call(
        functools.partial(_merge_kernel, final=final),
        grid=(n // tm,),
        in_specs=[row(D_MODEL), row(BRANCH_W), row(BRANCH_W), row(BRANCH_W),
                  row(N_BRANCH * BRANCH_W), row(N_BRANCH * D_MODEL),
                  pl.BlockSpec((N_BRANCH, BRANCH_W, D_MODEL), lambda i: (0, 0, 0)),
                  pl.BlockSpec((D_MODEL, D_MODEL), lambda i: (0, 0)),
                  pl.BlockSpec((1, D_MODEL), lambda i: (0, 0))],
        out_specs=row(D_MODEL),
        out_shape=jax.ShapeDtypeStruct((n, D_MODEL), F32),
        compiler_params=pltpu.CompilerParams(dimension_semantics=("arbitrary",),
                                             vmem_limit_bytes=VMEM_LIMIT),
        name="merge",
    )(x, oa, ob, oc, sz, gates, wbr, wout, gfin)


def _expand8(x):
    return jnp.concatenate([jnp.broadcast_to(x[h:h + 1, :], (SUBLANES, x.shape[1]))
                            for h in range(x.shape[0])], axis=0)


def _tile8(x, reps):
    return jnp.concatenate([x] * reps, axis=0)


def _new_key_bias(rows, n_new):
    t8 = lax.broadcasted_iota(jnp.int32, (rows, LANES), 0) % SUBLANES
    lane = lax.broadcasted_iota(jnp.int32, (rows, LANES), 1)
    return jnp.where(lane <= jnp.minimum(t8, n_new - 1), 0.0, NEG_INF).astype(F32)


def _init_state_d(m_sc, l_sc, acc_sc):
    m_sc[...] = jnp.full(m_sc.shape, NEG_INF, F32)
    l_sc[...] = jnp.zeros(l_sc.shape, F32)
    acc_sc[...] = jnp.zeros(acc_sc.shape, F32)


def _update_d(s, pv, m_sc, l_sc, acc_sc):
    m_prev = m_sc[...]
    m_new = jnp.maximum(m_prev, jnp.max(s, axis=1, keepdims=True))
    m_safe = jnp.where(m_new == NEG_INF, 0.0, m_new)
    p = jnp.exp(s - m_safe)
    alpha = jnp.exp(m_prev - m_safe)
    l_sc[...] = alpha * l_sc[...] + jnp.sum(p, axis=1, keepdims=True)
    acc_sc[...] = alpha * acc_sc[...] + pv(p.astype(BF16))
    m_sc[...] = m_new


def _page_logits(q, pages):
    return jnp.concatenate([_dot(q, pg[0].astype(BF16)) for pg in pages], axis=1)


def _page_pv(pages):
    def pv(p):
        acc = None
        for j, pg in enumerate(pages):
            t = _dot_nt(p[:, j * LANES:(j + 1) * LANES], pg[1].astype(BF16))
            acc = t if acc is None else acc + t
        return acc
    return pv


def _decode_state():
    rows = N_HEADS * SUBLANES
    return [pltpu.VMEM((rows, 1), F32), pltpu.VMEM((rows, 1), F32), pltpu.VMEM((rows, LANES), F32)]


def _dsa_decode_kernel(pt_ref, qi_ref, w_ref, q_ref, knew_idx_ref, knew_ref, vnew_ref, *rest,
                       n_pg, n_groups, n_sel, n_new, nbits):
    idx_pages = rest[:n_pg]
    kv_pages = rest[n_pg:2 * n_pg]
    o_ref, s_sc, thr_sc, cut_sc, m_sc, l_sc, acc_sc = rest[2 * n_pg:]
    ph = pl.program_id(1)
    g = pl.program_id(2)
    width = n_pg * LANES
    lane_w = lax.broadcasted_iota(jnp.int32, (SUBLANES, width), 1)

    def scores(r):
        r = jnp.maximum(r, 0.0) * w_ref[:, 0:1]
        acc = r[0:SUBLANES]
        for h in range(1, IDX_HEADS):
            acc = acc + r[h * SUBLANES:(h + 1) * SUBLANES]
        return _sortable_key(acc)

    @pl.when(ph == 0)
    def _():
        qi = qi_ref[:, 0:IDX_DIM]
        r = jnp.concatenate([_dot(qi, pg[...].astype(BF16)) for pg in idx_pages], axis=1)
        s_sc[g] = scores(r)

    @pl.when(jnp.logical_and(ph == 0, g == n_groups - 1))
    def _():
        key = scores(_dot_nt(qi_ref[...], knew_idx_ref[...]))
        visible = _new_key_bias(SUBLANES, n_new) == 0.0
        t8 = lax.broadcasted_iota(jnp.int32, (SUBLANES, LANES), 0)
        key = jnp.where(jnp.logical_and(visible, t8 < n_new), key, INT_MIN)
        s_sc[n_groups] = jnp.full((SUBLANES, width), INT_MIN, jnp.int32)
        s_sc[n_groups, :, 0:LANES] = key

        def counter(pred):
            def count(arg):
                acc = jnp.zeros((SUBLANES, width), F32)
                for gi in range(n_groups + 1):
                    acc = acc + jnp.where(pred(s_sc[gi], gi * width + lane_w, arg), 1.0, 0.0)
                return jnp.sum(acc, axis=1, keepdims=True)
            return count

        kf = float(n_sel)
        stat = (SUBLANES, 1)
        thr = _kth_largest(counter(lambda key, idx, x: key >= x), stat, kf)
        need = kf - counter(lambda key, idx, x: key > x)(thr)
        tie_lt = counter(lambda key, idx, x: jnp.logical_and(key == thr, idx < x))
        cut = _tie_cutoff(tie_lt, need, stat, nbits)
        thr_sc[...] = jnp.broadcast_to(thr, (SUBLANES, LANES))
        cut_sc[...] = jnp.broadcast_to(cut, (SUBLANES, LANES))
        _init_state_d(m_sc, l_sc, acc_sc)

    def sel_bias(key, idx):
        sel = _selected(key, idx, thr_sc[:, 0:1], cut_sc[:, 0:1])
        return _tile8(jnp.where(sel, 0.0, NEG_INF).astype(F32), N_HEADS)

    @pl.when(ph == 1)
    def _():
        s = _page_logits(q_ref[...], kv_pages) + sel_bias(s_sc[g], g * width + lane_w)
        _update_d(s, _page_pv(kv_pages), m_sc, l_sc, acc_sc)

    @pl.when(jnp.logical_and(ph == 1, g == n_groups - 1))
    def _():
        key = s_sc[n_groups, :, 0:LANES]
        s = _dot_nt(q_ref[...], knew_ref[...]) + sel_bias(key, n_groups * width + lane_w[:, 0:LANES])
        _update_d(s, lambda p: _dot(p, vnew_ref[...]), m_sc, l_sc, acc_sc)
        o_ref[...] = acc_sc[...] / l_sc[...]


def _page_specs(n_pg, block, index_of):
    return [pl.BlockSpec(block, functools.partial(index_of, j)) for j in range(n_pg)]


KV_PAGE = (None, 2, 2 * HEAD_DIM, LANES)


def _dsa_decode_call(page_table, layer_off, qi_rows, w_rows, q_rows, knew_idx, knew, vnew,
                     cache_idx, cache_kv, n_pg):
    nb, n_pages = page_table.shape
    n_groups = n_pages // n_pg
    n_new = 4
    n_sel = min(DSA_TOPK, (n_pages * LANES + n_new) // 4)
    nbits = max(1, ((n_groups + 1) * n_pg * LANES - 1).bit_length())

    def idx_map(j, b, ph, g, pt):
        gg = jnp.where(ph == 0, g, n_groups - 1)
        return (pt[b, gg * n_pg + j] + layer_off, 0, 0)

    def kv_map(j, b, ph, g, pt):
        gg = jnp.where(ph == 1, g, 0)
        return (pt[b, gg * n_pg + j] + layer_off, 0, 0, 0)

    per_b = lambda r, c: pl.BlockSpec((None, r, c), lambda b, ph, g, pt: (b, 0, 0))
    rows_q = N_HEADS * SUBLANES
    rows_i = IDX_HEADS * SUBLANES
    grid_spec = pltpu.PrefetchScalarGridSpec(
        num_scalar_prefetch=1,
        grid=(nb, 2, n_groups),
        in_specs=[per_b(rows_i, LANES), per_b(rows_i, LANES),
                  per_b(rows_q, LANES), per_b(LANES, LANES), per_b(LANES, LANES), per_b(LANES, LANES)]
        + _page_specs(n_pg, (None, IDX_DIM, LANES), idx_map)
        + _page_specs(n_pg, KV_PAGE, kv_map),
        out_specs=per_b(rows_q, LANES),
        scratch_shapes=[pltpu.VMEM((n_groups + 1, SUBLANES, n_pg * LANES), jnp.int32),
                        pltpu.VMEM((SUBLANES, LANES), jnp.int32), pltpu.VMEM((SUBLANES, LANES), jnp.int32)]
        + _decode_state(),
    )
    return pl.pallas_call(
        functools.partial(_dsa_decode_kernel, n_pg=n_pg, n_groups=n_groups, n_sel=n_sel,
                          n_new=n_new, nbits=nbits),
        grid_spec=grid_spec,
        out_shape=jax.ShapeDtypeStruct((nb, rows_q, LANES), F32),
        compiler_params=pltpu.CompilerParams(
            dimension_semantics=("arbitrary", "arbitrary", "arbitrary"), vmem_limit_bytes=VMEM_LIMIT),
        name="dsa_decode",
    )(page_table, qi_rows, w_rows, q_rows, knew_idx, knew, vnew,
      *([cache_idx] * n_pg), *([cache_kv] * n_pg))


def _moba_decode_kernel(pt_ref, q_ref, knew_ref, vnew_ref, *rest, n_pg, n_groups, n_sel, n_new):
    kv_pages = rest[:n_pg]
    o_ref, gate_sc, mb_sc, lb_sc, accb_sc, m_sc, l_sc, acc_sc = rest[n_pg:]
    g = pl.program_id(1)
    ppb = MOBA_BLOCK // LANES
    blocks_per_step = n_pg // ppb
    n_blocks = n_groups * blocks_per_step
    rows = N_HEADS * SUBLANES
    q = q_ref[...]

    for jb in range(blocks_per_step):
        pages = kv_pages[jb * ppb:(jb + 1) * ppb]
        ksum = None
        for pg in pages:
            t = jnp.sum(pg[0], axis=1, keepdims=True)
            ksum = t if ksum is None else ksum + t
        kmean = jnp.broadcast_to(ksum * (1.0 / MOBA_BLOCK), (LANES, LANES)).astype(BF16)
        gate = _dot(q, kmean)
        s = _page_logits(q, pages)
        mj = jnp.max(s, axis=1, keepdims=True)
        p = jnp.exp(s - mj)
        b = g * blocks_per_step + jb
        gate_sc[b] = gate
        mb_sc[b] = jnp.broadcast_to(mj, (rows, LANES))
        lb_sc[b] = jnp.broadcast_to(jnp.sum(p, axis=1, keepdims=True), (rows, LANES))
        accb_sc[b] = _page_pv(pages)(p.astype(BF16))

    @pl.when(g == n_groups - 1)
    def _():
        _init_state_d(m_sc, l_sc, acc_sc)
        s = _dot_nt(q, knew_ref[...]) + _new_key_bias(rows, n_new)
        _update_d(s, lambda p: _dot(p, vnew_ref[...]), m_sc, l_sc, acc_sc)
        m_own = jnp.broadcast_to(m_sc[...], (rows, LANES))

        def pick_round(_, c):
            def mx_body(b, mx):
                return jnp.maximum(mx, gate_sc[b])
            mx = lax.fori_loop(0, n_blocks, mx_body, jnp.full((rows, LANES), NEG_INF, F32))

            def first_body(b, first):
                return jnp.minimum(first, jnp.where(gate_sc[b] == mx, b, n_blocks))
            first = lax.fori_loop(0, n_blocks, first_body, jnp.full((rows, LANES), n_blocks, jnp.int32))

            def mark_body(b, cc):
                hit = jnp.logical_and(first == b, mx > NEG_INF)
                gate_sc[b] = jnp.where(hit, NEG_INF, gate_sc[b])
                lb_sc[b] = jnp.where(hit, -lb_sc[b], lb_sc[b])
                return cc
            lax.fori_loop(0, n_blocks, mark_body, 0)
            return c

        lax.fori_loop(0, n_sel, pick_round, 0)

        def max_body(b, mt):
            return jnp.maximum(mt, jnp.where(lb_sc[b] < 0.0, mb_sc[b], NEG_INF))
        m_tot = lax.fori_loop(0, n_blocks, max_body, m_own)

        def comb_body(b, carry):
            l_tot, acc_tot = carry
            picked = lb_sc[b] < 0.0
            wgt = jnp.where(picked, jnp.exp(mb_sc[b] - m_tot), 0.0)
            return l_tot - wgt * lb_sc[b], acc_tot + wgt * accb_sc[b]

        w_own = jnp.exp(m_own - m_tot)
        l0 = w_own * jnp.broadcast_to(l_sc[...], (rows, LANES))
        a0 = w_own * acc_sc[...]
        l_tot, acc_tot = lax.fori_loop(0, n_blocks, comb_body, (l0, a0))
        o_ref[...] = acc_tot / l_tot


def _moba_decode_call(page_table, layer_off, q_rows, knew, vnew, cache_kv, n_pg):
    nb, n_pages = page_table.shape
    n_groups = n_pages // n_pg
    n_blocks = n_pages * LANES // MOBA_BLOCK
    n_sel = min(MOBA_TOPK, n_blocks + 1)
    rows_q = N_HEADS * SUBLANES

    def kv_map(j, b, g, pt):
        return (pt[b, g * n_pg + j] + layer_off, 0, 0, 0)

    per_b = lambda r, c: pl.BlockSpec((None, r, c), lambda b, g, pt: (b, 0, 0))
    slab = lambda: pltpu.VMEM((n_blocks, rows_q, LANES), F32)
    grid_spec = pltpu.PrefetchScalarGridSpec(
        num_scalar_prefetch=1,
        grid=(nb, n_groups),
        in_specs=[per_b(rows_q, LANES), per_b(LANES, LANES), per_b(LANES, LANES)]
        + _page_specs(n_pg, KV_PAGE, kv_map),
        out_specs=per_b(rows_q, LANES),
        scratch_shapes=[slab(), slab(), slab(), slab()] + _decode_state(),
    )
    return pl.pallas_call(
        functools.partial(_moba_decode_kernel, n_pg=n_pg, n_groups=n_groups, n_sel=n_sel, n_new=4),
        grid_spec=grid_spec,
        out_shape=jax.ShapeDtypeStruct((nb, rows_q, LANES), F32),
        compiler_params=pltpu.CompilerParams(dimension_semantics=("arbitrary", "arbitrary"),
                                             vmem_limit_bytes=VMEM_LIMIT),
        name="moba_decode",
    )(page_table, q_rows, knew, vnew, *([cache_kv] * n_pg))


def _fox_decode_kernel(pt_ref, q_ref, knew_ref, vnew_ref, lfnew_rows_ref, lfnew_t_ref,
                       tri_ref, upper_ref, sfx_ref, *rest, n_pg, n_groups, n_new):
    kv_pages = rest[:n_pg]
    lf_pages = rest[n_pg:2 * n_pg]
    o_ref, cq_sc, carry_sc, m_sc, l_sc, acc_sc = rest[2 * n_pg:]
    g = pl.program_id(1)
    rows = N_HEADS * SUBLANES
    q = q_ref[...]

    @pl.when(g == 0)
    def _():
        cq_rows = _dot3_left(tri_ref[...], lfnew_rows_ref[...])
        cq_sc[...] = cq_rows
        new_cum_t = _dot3(lfnew_t_ref[...], upper_ref[...])
        carry_sc[...] = jnp.zeros(carry_sc.shape, F32)
        _init_state_d(m_sc, l_sc, acc_sc)
        s = (_dot_nt(q, knew_ref[...]) + cq_rows - _expand8(new_cum_t)
             + _new_key_bias(rows, n_new))
        _update_d(s, lambda p: _dot(p, vnew_ref[...]), m_sc, l_sc, acc_sc)

    carry = carry_sc[...]
    suffix = [None] * n_pg
    for j in range(n_pg - 1, -1, -1):
        lft = lf_pages[j][...]
        suffix[j] = _dot3(lft, sfx_ref[...]) + carry
        carry = carry + jnp.sum(lft, axis=1, keepdims=True)
    carry_sc[...] = carry
    bias = jnp.concatenate([_expand8(sf) for sf in suffix], axis=1)
    s = _page_logits(q, kv_pages) + bias + jnp.concatenate([cq_sc[...]] * n_pg, axis=1)
    _update_d(s, _page_pv(kv_pages), m_sc, l_sc, acc_sc)

    @pl.when(g == n_groups - 1)
    def _():
        o_ref[...] = acc_sc[...] / l_sc[...]


def _fox_decode_call(page_table, layer_off, q_rows, knew, vnew, lfnew_rows, lfnew_t,
                     cache_kv, cache_lft, n_pg):
    nb, n_pages = page_table.shape
    n_groups = n_pages // n_pg
    rows_q = N_HEADS * SUBLANES
    r = np.arange(rows_q)
    tri = ((r[:, None] // SUBLANES == r[None, :] // SUBLANES)
           & (r[None, :] % SUBLANES <= r[:, None] % SUBLANES)).astype(np.float32)
    lane = np.arange(LANES)
    upper = (lane[:, None] <= lane[None, :]).astype(np.float32)
    sfx = (lane[:, None] > lane[None, :]).astype(np.float32)

    def kv_map(j, b, g, pt):
        return (pt[b, (n_groups - 1 - g) * n_pg + j] + layer_off, 0, 0, 0)

    def lf_map(j, b, g, pt):
        return (pt[b, (n_groups - 1 - g) * n_pg + j] + layer_off, 0, 0)

    per_b = lambda rr, c: pl.BlockSpec((None, rr, c), lambda b, g, pt: (b, 0, 0))
    const = lambda rr, c: pl.BlockSpec((rr, c), lambda b, g, pt: (0, 0))
    grid_spec = pltpu.PrefetchScalarGridSpec(
        num_scalar_prefetch=1,
        grid=(nb, n_groups),
        in_specs=[per_b(rows_q, LANES), per_b(LANES, LANES), per_b(LANES, LANES),
                  per_b(rows_q, LANES), per_b(SUBLANES, LANES),
                  const(rows_q, rows_q), const(LANES, LANES), const(LANES, LANES)]
        + _page_specs(n_pg, KV_PAGE, kv_map)
        + _page_specs(n_pg, (None, SUBLANES, LANES), lf_map),
        out_specs=per_b(rows_q, LANES),
        scratch_shapes=[pltpu.VMEM((rows_q, LANES), F32), pltpu.VMEM((SUBLANES, LANES), F32)]
        + _decode_state(),
    )
    return pl.pallas_call(
        functools.partial(_fox_decode_kernel, n_pg=n_pg, n_groups=n_groups, n_new=4),
        grid_spec=grid_spec,
        out_shape=jax.ShapeDtypeStruct((nb, rows_q, LANES), F32),
        compiler_params=pltpu.CompilerParams(dimension_semantics=("arbitrary", "arbitrary"),
                                             vmem_limit_bytes=VMEM_LIMIT),
        name="fox_decode",
    )(page_table, q_rows, knew, vnew, lfnew_rows, lfnew_t,
      jnp.asarray(tri, BF16), jnp.asarray(upper, BF16), jnp.asarray(sfx, BF16),
      *([cache_kv] * n_pg), *([cache_lft] * n_pg))


def _rope_tables(pos):
    half = HEAD_DIM // 2
    freqs = ROPE_THETA ** (-jnp.arange(half, dtype=F32) / half)
    ang = pos.astype(F32)[:, None] * freqs[None, :]
    cos = jnp.tile(jnp.cos(ang), (1, LANES // half))
    sin = jnp.sin(ang)
    sin = jnp.tile(jnp.concatenate([-sin, sin], axis=1), (1, LANES // HEAD_DIM))
    return cos, sin


def _layout_w_in(w_in_l):
    cols = jnp.asarray(np.maximum(W_SRC, 0), jnp.int32)
    keep = jnp.asarray(W_SRC >= 0)
    return jnp.where(keep[None, :], jnp.take(w_in_l, cols, axis=1), 0.0).astype(BF16)


def _decode_rows(a, nb, t_new, heads):
    a = a.reshape(nb, t_new, heads, LANES).transpose(0, 2, 1, 3)
    a = jnp.pad(a, ((0, 0), (0, 0), (0, SUBLANES - t_new), (0, 0)))
    return a.reshape(nb, heads * SUBLANES, LANES)


def _new_rows(a, nb, t_new):
    a = a.reshape(nb, t_new, LANES)
    return jnp.pad(a, ((0, 0), (0, LANES - t_new), (0, 0)))


def _pick_pages_per_step(n_pages):
    for c in (16, 8, 4, 2):
        if n_pages % c == 0:
            return c
    raise ValueError("page count must be even")


def _value_tiles(kv, tk):
    n = kv.shape[0]
    v = kv[:, KV_W:2 * KV_W].reshape(n // tk, tk, N_KV_HEADS, HEAD_DIM).transpose(0, 2, 3, 1)
    tail = jnp.zeros((n // tk, N_KV_HEADS, VT_ROWS - HEAD_DIM, tk), BF16).at[:, :, 0, :].set(1.0)
    return jnp.concatenate([v, tail], axis=2)


def _prompt_branches(proj, batch, t_len):
    (qa, qb, qi, qc, kva, kvb, kvc, kidx, _, _, bkv, _, _, misc, _, _) = proj
    tq = MOBA_BLOCK
    n = qa.shape[0]
    n_blocks = t_len // MOBA_BLOCK
    o_a = _dsa_prompt_call(qa.T, qi.T, misc.T, kidx, kva, _value_tiles(kva, tq), batch, t_len, tq)
    kmean = _kmean_call(bkv, batch * n_blocks).reshape(batch, n_blocks, LANES)
    kmean = jnp.pad(kmean, ((0, 0), (0, LANES - n_blocks), (0, 0))).astype(BF16)
    o_b = _moba_prompt_call(qb.T, kmean, kvb, _value_tiles(kvb, tq), batch, t_len)
    kaug, qaug = _foxprep_call(misc, kvc, batch, t_len, LANES)
    qpt = jnp.concatenate([qc.T.reshape(N_HEADS, LANES, n), qaug.T.reshape(N_HEADS, LANES, n)], axis=1)
    o_c = _fox_prompt_call(qpt, kaug, _value_tiles(kvc, tq), batch, t_len, tq)
    return o_a.T, o_b.T, o_c.T


def _sample_branches(proj, nb, t_new, page_table, layer_off, caches):
    (qa, qb, qi, qc, kva, kvb, kvc, kidx, _, _, _, _, logf, misc, _, _) = proj
    ca_idx, ca_kv, cb_kv, cc_kv, cc_lft = caches
    n_pg = _pick_pages_per_step(page_table.shape[1])

    def head_rows(col, heads):
        a = col.reshape(nb, t_new, heads).transpose(0, 2, 1)
        a = jnp.pad(a, ((0, 0), (0, 0), (0, SUBLANES - t_new)))
        return jnp.broadcast_to(a.reshape(nb, heads * SUBLANES, 1), (nb, heads * SUBLANES, LANES))

    def new_kv(kv):
        return _new_rows(kv[:, 0:LANES], nb, t_new), _new_rows(kv[:, LANES:2 * LANES], nb, t_new)

    d_a = _dsa_decode_call(
        page_table, layer_off, _decode_rows(qi, nb, t_new, IDX_HEADS),
        head_rows(misc[:, MISC_W:MISC_W + IDX_HEADS], IDX_HEADS),
        _decode_rows(qa, nb, t_new, N_HEADS), _new_rows(kidx, nb, t_new), *new_kv(kva),
        ca_idx, ca_kv, n_pg)
    d_b = _moba_decode_call(page_table, layer_off, _decode_rows(qb, nb, t_new, N_HEADS),
                            *new_kv(kvb), cb_kv, n_pg)
    lf_t = jnp.pad(logf.reshape(nb, t_new, N_HEADS).transpose(0, 2, 1),
                   ((0, 0), (0, 0), (0, LANES - t_new)))
    d_c = _fox_decode_call(page_table, layer_off, _decode_rows(qc, nb, t_new, N_HEADS),
                           *new_kv(kvc), head_rows(logf, N_HEADS), lf_t, cc_kv, cc_lft, n_pg)

    def compact(d):
        d = d.reshape(nb, N_KV_HEADS, GROUP, SUBLANES, N_KV_HEADS, HEAD_DIM)[:, :, :, :t_new]
        d = jnp.stack([d[:, gidx, :, :, gidx, :] for gidx in range(N_KV_HEADS)], axis=1)
        return d.transpose(0, 3, 1, 2, 4).reshape(nb * t_new, BRANCH_W).astype(BF16)

    return compact(d_a), compact(d_b), compact(d_c)


def _native_pages(cache, depth, n_pool):
    nd = cache.ndim
    c = jnp.transpose(cache, (0, 1) + tuple(range(3, nd)) + (2,))
    return c.reshape((depth * n_pool,) + c.shape[2:])


def kernel(x_prompt, x_sample, cache_a_kv, cache_a_idx, cache_b_kv, cache_c_kv, cache_c_logf,
           page_table, norm_g, w_in, b_forget, w_branch, w_out, final_norm_g):
    batch, t_len, _ = x_prompt.shape
    nb, t_new, _ = x_sample.shape
    depth = norm_g.shape[0]
    n_pool, page_size = cache_a_kv.shape[1], cache_a_kv.shape[2]
    n_pages = page_table.shape[1]
    past_len = n_pages * page_size
    assert page_size == LANES and t_len % MOBA_BLOCK == 0 and past_len % MOBA_BLOCK == 0 and t_new == 4
    n_p = batch * t_len
    n_s = nb * t_new

    pos_p = jnp.tile(jnp.arange(t_len, dtype=jnp.int32), batch)
    pos_s = jnp.tile(past_len + jnp.arange(t_new, dtype=jnp.int32), nb)
    cos_p, sin_p = _rope_tables(pos_p)
    cos_s, sin_s = _rope_tables(pos_s)

    kv_pages = lambda c: _native_pages(c, depth, n_pool).reshape(
        depth * n_pool, 2, N_KV_HEADS * HEAD_DIM, page_size)
    ca_kv, cb_kv, cc_kv = kv_pages(cache_a_kv), kv_pages(cache_b_kv), kv_pages(cache_c_kv)
    ca_idx = _native_pages(cache_a_idx, depth, n_pool)
    cc_lft = _native_pages(cache_c_logf, depth, n_pool)

    xp = x_prompt.reshape(n_p, D_MODEL)
    xs = x_sample.reshape(n_s, D_MODEL)
    gfin = final_norm_g.reshape(1, D_MODEL)
    rows_p = []
    rows_s = []
    for l in range(depth):
        w = _layout_w_in(w_in[l])
        g = norm_g[l].reshape(1, D_MODEL)
        bf_pad = jnp.zeros((1, LANES), F32).at[0, MISC_F:MISC_F + N_HEADS].set(b_forget[l])
        wbr = w_branch[l].astype(BF16)
        wout = w_out[l].astype(BF16)
        final = l == depth - 1

        proj_p = _proj_call(xp, g, w, cos_p, sin_p, bf_pad, 256)
        rows_p.append(proj_p[8:13])
        o_a, o_b, o_c = _prompt_branches(proj_p, batch, t_len)
        xp = _merge_call(xp, o_a, o_b, o_c, proj_p[14], proj_p[15], wbr, wout, gfin, 256, final)

        proj_s = _proj_call(xs, g, w, cos_s, sin_s, bf_pad, n_s)
        rows_s.append(proj_s[8:13])
        o_a, o_b, o_c = _sample_branches(proj_s, nb, t_new, page_table, l * n_pool,
                                         (ca_idx, ca_kv, cb_kv, cc_kv, cc_lft))
        xs = _merge_call(xs, o_a, o_b, o_c, proj_s[14], proj_s[15], wbr, wout, gfin, n_s, final)

    def stack(rows, i, b, t, tail):
        return jnp.stack([r[i] for r in rows]).reshape((depth, b, t) + tail)

    kv_tail = (2, N_KV_HEADS, HEAD_DIM)
    outs_p = (stack(rows_p, 0, batch, t_len, kv_tail), stack(rows_p, 1, batch, t_len, (IDX_DIM,)),
              stack(rows_p, 2, batch, t_len, kv_tail), stack(rows_p, 3, batch, t_len, kv_tail),
              stack(rows_p, 4, batch, t_len, (N_HEADS,)))
    outs_s = (stack(rows_s, 0, nb, t_new, kv_tail), stack(rows_s, 1, nb, t_new, (IDX_DIM,)),
              stack(rows_s, 2, nb, t_new, kv_tail), stack(rows_s, 3, nb, t_new, kv_tail),
              stack(rows_s, 4, nb, t_new, (N_HEADS,)))
    return (xp.reshape(batch, t_len, D_MODEL), xs.reshape(nb, t_new, D_MODEL)) + outs_p + outs_s
```

```python
import functools

import numpy as np
import jax
import jax.numpy as jnp
from jax import lax
from jax.experimental import pallas as pl
from jax.experimental.pallas import tpu as pltpu

D_MODEL = 1024
HEAD_DIM = 64
N_HEADS = 8
N_KV_HEADS = 2
GROUP = N_HEADS // N_KV_HEADS
BRANCH_W = N_HEADS * HEAD_DIM
KV_W = N_KV_HEADS * HEAD_DIM
N_BRANCH = 3
IDX_HEADS = 4
IDX_DIM = 64
DSA_TOPK = 256
MOBA_BLOCK = 256
MOBA_TOPK = 3
ROPE_THETA = 10000.0
RMS_EPS = 1e-6
ATTN_SCALE = HEAD_DIM ** -0.5
IDX_SCALE = (IDX_HEADS ** -0.5) * (IDX_DIM ** -0.5)
LOG2E = float(np.log2(np.e))
QK_SCALE = ATTN_SCALE * LOG2E

LANES = 128
SUBLANES = 8
VMEM_LIMIT = 56 * 1024 * 1024
INT_MIN = -2 ** 31
NEG_INF = float("-inf")
F32 = jnp.float32
BF16 = jnp.bfloat16

VT_ROWS = 80
VT_ONE = HEAD_DIM

IN_SPLITS = (
    ('a_q', BRANCH_W), ('a_k', KV_W), ('a_v', KV_W), ('a_z', BRANCH_W),
    ('a_iq', IDX_HEADS * IDX_DIM), ('a_ik', IDX_DIM), ('a_iw', IDX_HEADS),
    ('b_q', BRANCH_W), ('b_k', KV_W), ('b_v', KV_W), ('b_z', BRANCH_W),
    ('c_q', BRANCH_W), ('c_k', KV_W), ('c_v', KV_W), ('c_z', BRANCH_W), ('c_f', N_HEADS),
    ('gates', N_BRANCH * D_MODEL),
)
IN_OFF = {}
_o = 0
for _n, _w in IN_SPLITS:
    IN_OFF[_n] = _o
    _o += _w
IN_WIDTH = _o

MISC_F = 0
MISC_W = 8
AUG_ONE = 0
AUG_C = 3


def _wide_q_cols(name):
    src = np.full((N_HEADS * LANES,), -1, np.int64)
    for h in range(N_HEADS):
        g = h // GROUP
        dst = h * LANES + g * HEAD_DIM
        src[dst:dst + HEAD_DIM] = IN_OFF[name] + h * HEAD_DIM + np.arange(HEAD_DIM)
    return src


def _seg(name, width, pad_to=None):
    src = IN_OFF[name] + np.arange(width)
    if pad_to is not None and pad_to > width:
        src = np.concatenate([src, np.full((pad_to - width,), -1, np.int64)])
    return src


def _build_layout():
    segs = []
    segs.append(('a_qw', _wide_q_cols('a_q')))
    segs.append(('b_qw', _wide_q_cols('b_q')))
    iq = np.full((IDX_HEADS * LANES,), -1, np.int64)
    for h in range(IDX_HEADS):
        iq[h * LANES:h * LANES + IDX_DIM] = IN_OFF['a_iq'] + h * IDX_DIM + np.arange(IDX_DIM)
    segs.append(('a_iqw', iq))
    segs.append(('a_k', _seg('a_k', KV_W)))
    segs.append(('b_k', _seg('b_k', KV_W)))
    segs.append(('a_ik', _seg('a_ik', IDX_DIM, LANES)))
    segs.append(('a_v', _seg('a_v', KV_W)))
    segs.append(('b_v', _seg('b_v', KV_W)))
    segs.append(('c_qw', _wide_q_cols('c_q')))
    segs.append(('c_k', _seg('c_k', KV_W)))
    segs.append(('c_v', _seg('c_v', KV_W)))
    segs.append(('a_z', _seg('a_z', BRANCH_W)))
    segs.append(('b_z', _seg('b_z', BRANCH_W)))
    segs.append(('c_z', _seg('c_z', BRANCH_W)))
    misc = np.full((LANES,), -1, np.int64)
    misc[MISC_F:MISC_F + N_HEADS] = IN_OFF['c_f'] + np.arange(N_HEADS)
    misc[MISC_W:MISC_W + IDX_HEADS] = IN_OFF['a_iw'] + np.arange(IDX_HEADS)
    segs.append(('misc', misc))
    segs.append(('gates', _seg('gates', N_BRANCH * D_MODEL)))
    off = {}
    o = 0
    for n, s in segs:
        off[n] = (o, len(s))
        o += len(s)
    return np.concatenate([s for _, s in segs]), off, o


W_SRC, W_OFF, W_TOTAL = _build_layout()


def _dot(a, b):
    return jnp.dot(a, b, preferred_element_type=F32)


def _dot_nt(a, b):
    return lax.dot_general(a, b, (((1,), (1,)), ((), ())), preferred_element_type=F32)


def _split3(x):
    hi = x.astype(BF16)
    r1 = x - hi.astype(F32)
    mid = r1.astype(BF16)
    lo = (r1 - mid.astype(F32)).astype(BF16)
    return hi, mid, lo


def _dot3(x, m_bf16):
    hi, mid, lo = _split3(x)
    return _dot(hi, m_bf16) + _dot(mid, m_bf16) + _dot(lo, m_bf16)


def _dot3_left(m_bf16, x):
    hi, mid, lo = _split3(x)
    return _dot(m_bf16, hi) + _dot(m_bf16, mid) + _dot(m_bf16, lo)


def _proj_kernel(x_ref, g_ref, w_ref, cos_ref, sin_ref, bf_ref,
                 qa_ref, qb_ref, qi_ref, qc_ref, kva_ref, kvb_ref, kvc_ref, kidx_ref,
                 akv_ref, aidx_ref, bkv_ref, ckv_ref, logf_ref, misc_ref, sz_ref, gate_ref):
    x = x_ref[...]
    ms = jnp.mean(x * x, axis=-1, keepdims=True)
    h = (x * lax.rsqrt(ms + RMS_EPS) * g_ref[...]).astype(BF16)
    cos = cos_ref[...]
    sin = sin_ref[...]
    tm = x.shape[0]
    lane = lax.broadcasted_iota(jnp.int32, (tm, LANES), 1)
    first_half = (lane % HEAD_DIM) < (HEAD_DIM // 2)

    def proj(name):
        o, w = W_OFF[name]
        return _dot(h, w_ref[:, o:o + w])

    def rope(y):
        partner = jnp.where(first_half, pltpu.roll(y, LANES - HEAD_DIM // 2, 1),
                            pltpu.roll(y, HEAD_DIM // 2, 1))
        return y * cos + partner * sin

    for name, ref in (('a_qw', qa_ref), ('b_qw', qb_ref)):
        y = proj(name)
        for c in range(N_HEADS):
            ref[:, c * LANES:(c + 1) * LANES] = (
                rope(y[:, c * LANES:(c + 1) * LANES]) * QK_SCALE).astype(BF16)
    y = proj('a_iqw')
    for c in range(IDX_HEADS):
        qi_ref[:, c * LANES:(c + 1) * LANES] = rope(y[:, c * LANES:(c + 1) * LANES]).astype(BF16)
    y = proj('c_qw')
    qc_ref[...] = (y * QK_SCALE).astype(BF16)

    for kname, vname, f32_ref, bf_out in (('a_k', 'a_v', akv_ref, kva_ref),
                                          ('b_k', 'b_v', bkv_ref, kvb_ref),
                                          ('c_k', 'c_v', ckv_ref, kvc_ref)):
        k = proj(kname)
        if kname != 'c_k':
            k = rope(k)
        v = proj(vname)
        f32_ref[:, 0:LANES] = k
        f32_ref[:, LANES:2 * LANES] = v
        bf_out[:, 0:LANES] = k.astype(BF16)
        bf_out[:, LANES:2 * LANES] = v.astype(BF16)

    ik = rope(proj('a_ik'))
    aidx_ref[...] = ik[:, 0:IDX_DIM]
    kidx_ref[...] = ik.astype(BF16)

    for i, name in enumerate(('a_z', 'b_z', 'c_z')):
        z = proj(name)
        sz_ref[:, i * BRANCH_W:(i + 1) * BRANCH_W] = (z * jax.nn.sigmoid(z)).astype(BF16)

    y = proj('misc')
    t = y + bf_ref[...]
    logf = jnp.minimum(t, 0.0) - jnp.log1p(jnp.exp(-jnp.abs(t)))
    misc = jnp.where(lane < MISC_W, logf, y * IDX_SCALE)
    misc_ref[...] = misc
    logf_ref[...] = misc[:, MISC_F:MISC_F + N_HEADS]

    y = proj('gates')
    gate_ref[...] = jax.nn.sigmoid(y).astype(BF16)


def _proj_call(x, g, w, cos, sin, bf_pad, tm):
    n = x.shape[0]
    row = lambda width: pl.BlockSpec((tm, width), lambda i: (i, 0))
    const = lambda shape: pl.BlockSpec(shape, lambda i: (0, 0))
    out_shapes = [
        ((n, N_HEADS * LANES), BF16),
        ((n, N_HEADS * LANES), BF16),
        ((n, IDX_HEADS * LANES), BF16),
        ((n, N_HEADS * LANES), BF16),
        ((n, 2 * KV_W), BF16),
        ((n, 2 * KV_W), BF16),
        ((n, 2 * KV_W), BF16),
        ((n, LANES), BF16),
        ((n, 2 * KV_W), F32),
        ((n, IDX_DIM), F32),
        ((n, 2 * KV_W), F32),
        ((n, 2 * KV_W), F32),
        ((n, N_HEADS), F32),
        ((n, LANES), F32),
        ((n, N_BRANCH * BRANCH_W), BF16),
        ((n, N_BRANCH * D_MODEL), BF16),
    ]
    return pl.pallas_call(
        _proj_kernel,
        grid=(n // tm,),
        in_specs=[row(D_MODEL), const((1, D_MODEL)),
                  pl.BlockSpec((D_MODEL, W_TOTAL), lambda i: (0, 0), pipeline_mode=pl.Buffered(1)),
                  row(LANES), row(LANES), const((1, LANES))],
        out_specs=[row(s[1]) for s, _ in out_shapes],
        out_shape=[jax.ShapeDtypeStruct(s, d) for s, d in out_shapes],
        compiler_params=pltpu.CompilerParams(dimension_semantics=("arbitrary",),
                                             vmem_limit_bytes=VMEM_LIMIT),
        name="proj",
    )(x, g, w, cos, sin, bf_pad)


def _rows8(x):
    return x.reshape(x.shape[0] // SUBLANES, SUBLANES, x.shape[1])


def _allreduce8(x8, op):
    for shift in (4, 2, 1):
        x8 = op(x8, pltpu.roll(x8, shift, 0))
    return x8


def _init_state_t(m_sc, acc_sc):
    m_sc[...] = jnp.full(m_sc.shape, NEG_INF, F32)
    acc_sc[...] = jnp.zeros(acc_sc.shape, F32)


CHAIN = 8
N_CHAINS = N_HEADS // CHAIN


def _update_t(c, s, vt_tile, m_sc, acc_sc):
    s3 = _rows8(s)
    m_prev = m_sc[c]
    m_new = jnp.maximum(m_prev, _allreduce8(jnp.max(s3, axis=0), jnp.maximum))
    m_safe = jnp.where(m_new == NEG_INF, 0.0, m_new)
    p = jnp.exp2(s3 - m_safe[None]).reshape(s.shape).astype(BF16)
    alpha = jnp.exp2(m_prev - m_safe)
    acc = acc_sc[c]
    per_group = GROUP * (s.shape[1] // CHAIN)
    first = (c * CHAIN) // GROUP
    pv = [_dot(vt_tile[first + i], p[:, i * per_group:(i + 1) * per_group])
          for i in range(max(1, CHAIN // GROUP))]
    pv = pv[0] if len(pv) == 1 else jnp.concatenate(pv, axis=1)
    acc_sc[c] = (_rows8(acc) * alpha[None]).reshape(acc.shape) + pv
    m_sc[c] = m_new


def _group_queries(qg_sc, head_block, tq):
    for h in range(N_HEADS):
        c, j = divmod(h, CHAIN)
        qg_sc[c, :, j * tq:(j + 1) * tq] = head_block(h)


def _per_head(x):
    return jnp.concatenate([x] * CHAIN, axis=1)


def _write_t(o_ref, acc_sc, tq):
    for h in range(N_HEADS):
        c, j = divmod(h, CHAIN)
        acc = acc_sc[c, :, j * tq:(j + 1) * tq]
        o_ref[h * HEAD_DIM:(h + 1) * HEAD_DIM, :] = (
            acc[0:HEAD_DIM] / acc[VT_ONE:VT_ONE + 1]).astype(o_ref.dtype)


def _for_tiles(n, tile):
    def body(i, c):
        tile(2 * i)
        tile(2 * i + 1)
        return c

    lax.fori_loop(0, n // 2, body, 0)

    @pl.when(n % 2 == 1)
    def _():
        tile(n - 1)


def _causal_bias_t(tk, tq):
    key = lax.broadcasted_iota(jnp.int32, (tk, tq), 0)
    qry = lax.broadcasted_iota(jnp.int32, (tk, tq), 1)
    return jnp.where(key <= qry, 0.0, NEG_INF).astype(F32)


def _attn_scratch(tq, dk):
    return [pltpu.VMEM((N_CHAINS, dk, CHAIN * tq), BF16),
            pltpu.VMEM((N_CHAINS, SUBLANES, CHAIN * tq), F32),
            pltpu.VMEM((N_CHAINS, VT_ROWS, CHAIN * tq), F32)]


def _prompt_params():
    return pltpu.CompilerParams(dimension_semantics=("arbitrary", "arbitrary"),
                                vmem_limit_bytes=VMEM_LIMIT)


def _fox_aug_mats():
    pk = np.zeros((3, LANES, LANES), np.float32)
    pq = np.zeros((3, LANES, N_HEADS * LANES), np.float32)
    kconst = np.zeros((1, LANES), np.float32)
    qconst = np.zeros((1, N_HEADS * LANES), np.float32)
    for t in range(3):
        kconst[0, AUG_ONE + t] = 1.0
        for h in range(N_HEADS):
            pk[t, MISC_F + h, AUG_C + 3 * h + t] = -1.0
            pq[t, MISC_F + h, h * LANES + AUG_ONE + t] = 1.0
            qconst[0, h * LANES + AUG_C + 3 * h + t] = 1.0
    return pk, pq, kconst, qconst


def _foxprep_kernel(misc_ref, kv_ref, ltri_ref, pk_ref, pq_ref, kconst_ref, qconst_ref,
                    kaug_ref, qaug_ref, carry_sc):
    @pl.when(pl.program_id(1) == 0)
    def _():
        carry_sc[...] = jnp.zeros(carry_sc.shape, F32)

    lf = misc_ref[...]
    rows = lf.shape[0]
    c = _dot3_left(ltri_ref[...], lf) + carry_sc[0:1, :]
    carry_sc[0:1, :] = c[rows - 1:rows, :]
    ka = kconst_ref[...]
    qa = qconst_ref[...]
    for t, part in enumerate(_split3(c * LOG2E)):
        ka = ka + _dot(part, pk_ref[t])
        qa = qa + _dot(part, pq_ref[t])
    kaug_ref[:, 0:LANES] = kv_ref[...]
    kaug_ref[:, LANES:2 * LANES] = ka.astype(BF16)
    qaug_ref[...] = qa.astype(BF16)


def _foxprep_call(misc, kvc, batch, t_len, chunk):
    n = misc.shape[0]
    nch = t_len // chunk
    pk, pq, kconst, qconst = _fox_aug_mats()
    ltri = np.tril(np.ones((chunk, chunk), np.float32))
    rowblk = lambda width, col=0: pl.BlockSpec((chunk, width), lambda b, i: (b * nch + i, col))
    const = lambda shape: pl.BlockSpec(shape, lambda b, i: tuple(0 for _ in shape))
    return pl.pallas_call(
        _foxprep_kernel,
        grid=(batch, nch),
        in_specs=[rowblk(LANES), rowblk(LANES), const((chunk, chunk)), const((3, LANES, LANES)),
                  const((3, LANES, N_HEADS * LANES)), const((1, LANES)), const((1, N_HEADS * LANES))],
        out_specs=[rowblk(2 * LANES), rowblk(N_HEADS * LANES)],
        out_shape=[jax.ShapeDtypeStruct((n, 2 * LANES), BF16),
                   jax.ShapeDtypeStruct((n, N_HEADS * LANES), BF16)],
        scratch_shapes=[pltpu.VMEM((SUBLANES, LANES), F32)],
        compiler_params=_prompt_params(),
        name="foxprep",
    )(misc, kvc, jnp.asarray(ltri, BF16), jnp.asarray(pk, BF16), jnp.asarray(pq, BF16),
      jnp.asarray(kconst, F32), jnp.asarray(qconst, F32))


def _fox_prompt_kernel(qt_ref, kaug_ref, vt_ref, o_ref, qg_sc, m_sc, acc_sc, *, tq, tk):
    qi = pl.program_id(1)
    _init_state_t(m_sc, acc_sc)
    _group_queries(qg_sc, lambda h: qt_ref[h], tq)

    def tile(ki, bias):
        kt = kaug_ref[pl.ds(pl.multiple_of(ki * tk, tk), tk), :]
        for c in range(N_CHAINS):
            s = _dot(kt, qg_sc[c])
            if bias is not None:
                s = s + bias
            _update_t(c, s, vt_ref.at[ki], m_sc, acc_sc)

    _for_tiles(qi, lambda ki: tile(ki, None))
    tile(qi, _per_head(_causal_bias_t(tk, tq)))
    _write_t(o_ref, acc_sc, tq)


def _vt_spec(nk, tk):
    return pl.BlockSpec((nk, N_KV_HEADS, VT_ROWS, tk), lambda b, i: (b, 0, 0, 0))


def _fox_prompt_call(qpt, kaug, vt, batch, t_len, tq):
    n = kaug.shape[0]
    nq = t_len // tq
    return pl.pallas_call(
        functools.partial(_fox_prompt_kernel, tq=tq, tk=tq),
        grid=(batch, nq),
        in_specs=[pl.BlockSpec((N_HEADS, 2 * LANES, tq), lambda b, i: (0, 0, b * nq + i)),
                  pl.BlockSpec((t_len, 2 * LANES), lambda b, i: (b, 0)),
                  _vt_spec(nq, tq)],
        out_specs=pl.BlockSpec((BRANCH_W, tq), lambda b, i: (0, b * nq + i)),
        out_shape=jax.ShapeDtypeStruct((BRANCH_W, n), BF16),
        scratch_shapes=_attn_scratch(tq, 2 * LANES),
        compiler_params=_prompt_params(),
        name="fox_prompt",
    )(qpt, kaug, vt)


def _sortable_key(score):
    bits = lax.bitcast_convert_type(score + 0.0, jnp.int32)
    return jnp.where(bits < 0, bits ^ jnp.int32(0x7FFFFFFF), bits)


def _selected(key, idx, thr, cutoff):
    sel = jnp.logical_or(key > thr, jnp.logical_and(key == thr, idx <= cutoff))
    return jnp.logical_and(sel, key != INT_MIN)


def _kth_largest(count_ge, shape, k):
    unknown = float(2 ** 30)
    c0 = count_ge(jnp.zeros(shape, jnp.int32))
    ans = jnp.where(c0 >= k, 0, INT_MIN).astype(jnp.int32)
    cnt = jnp.where(c0 >= k, c0, unknown)

    def is_open(cnt):
        return (jnp.max(cnt) > k).astype(jnp.int32)

    def step(i, state):
        ans, cnt = state
        cand = ans + lax.shift_left(jnp.int32(1), jnp.int32(30) - i)
        c = count_ge(cand)
        take = c >= k
        return jnp.where(take, cand, ans), jnp.where(take, c, cnt)

    n_blind = 15
    ans, cnt = lax.fori_loop(0, n_blind, step, (ans, cnt))

    def cond(state):
        i, _, _, still_open = state
        return jnp.logical_and(i < 31, still_open > 0)

    def body(state):
        i, ans, cnt, _ = state
        ans, cnt = step(i + 1, step(i, (ans, cnt)))
        return i + 2, ans, cnt, is_open(cnt)

    _, ans, cnt, _ = lax.while_loop(cond, body, (jnp.int32(n_blind), ans, cnt, is_open(cnt)))
    return ans, cnt


def _tie_cutoff(count_tie_lt, need, shape, nbits):
    def body(i, x):
        cand = x + lax.shift_left(jnp.int32(1), jnp.int32(nbits - 1) - i)
        return jnp.where(count_tie_lt(cand) < need, cand, x)

    return lax.fori_loop(0, nbits, body, jnp.zeros(shape, jnp.int32))


def _dsa_prompt_kernel(qt_ref, qit_ref, misct_ref, kidx_ref, k_ref, vt_ref, o_ref,
                       s_sc, qig_sc, qg_sc, m_sc, acc_sc, *, tq, tk, n_sel, nbits):
    qi = pl.program_id(1)
    nk = qi + 1
    key_row = lax.broadcasted_iota(jnp.int32, (tk, tq), 0)
    qry_col = lax.broadcasted_iota(jnp.int32, (tk, tq), 1)
    row3 = _rows8(key_row)
    for h in range(IDX_HEADS):
        qig_sc[:, h * tq:(h + 1) * tq] = qit_ref[h * LANES:(h + 1) * LANES, :]
    wrow = jnp.concatenate([misct_ref[MISC_W + h:MISC_W + h + 1, :] for h in range(IDX_HEADS)], axis=1)

    def score_tile(ki):
        kt = kidx_ref[pl.ds(pl.multiple_of(ki * tk, tk), tk), :]
        r = jnp.maximum(_dot(kt, qig_sc[...]), 0.0) * wrow
        acc = r[:, 0:tq]
        for h in range(1, IDX_HEADS):
            acc = acc + r[:, h * tq:(h + 1) * tq]
        return _sortable_key(acc)

    def fill(ki, c):
        s_sc[ki] = score_tile(ki)
        return c

    lax.fori_loop(0, qi, fill, 0)
    s_sc[qi] = jnp.where(key_row <= qry_col, score_tile(qi), INT_MIN)

    def counter(pred):
        def count(arg):
            def body(ki, acc):
                hit = pred(_rows8(s_sc[ki]), ki * tk + row3, arg[None])
                return acc + jnp.sum(jnp.where(hit, 1.0, 0.0), axis=0)
            acc = lax.fori_loop(0, nk, body, jnp.zeros((SUBLANES, tq), F32))
            return _allreduce8(acc, jnp.add)
        return count

    kf = float(n_sel)
    stat = (SUBLANES, tq)
    thr, cnt_ge = _kth_largest(counter(lambda key, idx, x: key >= x), stat, kf)

    def resolve_ties():
        need = kf - counter(lambda key, idx, x: key > x)(thr)
        tie_lt = counter(lambda key, idx, x: jnp.logical_and(key == thr[None], idx < x))
        return _tie_cutoff(tie_lt, need, stat, nbits)

    cutoff = lax.cond(jnp.max(cnt_ge) > kf, resolve_ties,
                      lambda: jnp.full(stat, 2 ** 30, jnp.int32))

    def to_bias(ki, c):
        sel = _selected(_rows8(s_sc[ki]), ki * tk + row3, thr[None], cutoff[None])
        bias = jnp.where(sel, 0.0, NEG_INF).astype(F32).reshape(tk, tq)
        s_sc[ki] = lax.bitcast_convert_type(bias, jnp.int32)
        return c

    lax.fori_loop(0, nk, to_bias, 0)

    _init_state_t(m_sc, acc_sc)
    _group_queries(qg_sc, lambda h: qt_ref[h * LANES:(h + 1) * LANES, :], tq)

    def attend(ki):
        bias = _per_head(lax.bitcast_convert_type(s_sc[ki], F32))
        kt = k_ref[pl.ds(pl.multiple_of(ki * tk, tk), tk), :]
        for c in range(N_CHAINS):
            _update_t(c, _dot(kt, qg_sc[c]) + bias, vt_ref.at[ki], m_sc, acc_sc)

    _for_tiles(nk, attend)
    _write_t(o_ref, acc_sc, tq)


def _dsa_prompt_call(qat, qit, misct, kidx, kva, vt, batch, t_len, tq):
    n = kidx.shape[0]
    nq = t_len // tq
    n_sel = min(DSA_TOPK, t_len // 4)
    nbits = max(1, (t_len - 1).bit_length())
    colblk = lambda rows: pl.BlockSpec((rows, tq), lambda b, i: (0, b * nq + i))
    seq = lambda width: pl.BlockSpec((t_len, width), lambda b, i: (b, 0))
    return pl.pallas_call(
        functools.partial(_dsa_prompt_kernel, tq=tq, tk=tq, n_sel=n_sel, nbits=nbits),
        grid=(batch, nq),
        in_specs=[colblk(N_HEADS * LANES), colblk(IDX_HEADS * LANES), colblk(LANES),
                  seq(LANES), seq(LANES), _vt_spec(nq, tq)],
        out_specs=colblk(BRANCH_W),
        out_shape=jax.ShapeDtypeStruct((BRANCH_W, n), BF16),
        scratch_shapes=[pltpu.VMEM((nq, tq, tq), jnp.int32),
                        pltpu.VMEM((LANES, IDX_HEADS * tq), BF16)] + _attn_scratch(tq, LANES),
        compiler_params=_prompt_params(),
        name="dsa_prompt",
    )(qat, qit, misct, kidx, kva, vt)


def _kmean_kernel(k_ref, o_ref):
    o_ref[...] = jnp.mean(k_ref[...], axis=0, keepdims=True)


def _kmean_call(kv_f32, n_blocks):
    return pl.pallas_call(
        _kmean_kernel,
        grid=(n_blocks,),
        in_specs=[pl.BlockSpec((MOBA_BLOCK, LANES), lambda i: (i, 0))],
        out_specs=pl.BlockSpec((None, 1, LANES), lambda i: (i, 0, 0)),
        out_shape=jax.ShapeDtypeStruct((n_blocks, 1, LANES), F32),
        name="moba_kmean",
    )(kv_f32)


def _colmax(x):
    return _allreduce8(jnp.max(_rows8(x), axis=0), jnp.maximum)


def _top_blocks_bias_t(gate, n_valid_lt, n_sel):
    blk_i = lax.broadcasted_iota(jnp.int32, gate.shape, 0)
    blk = blk_i.astype(F32)
    cur3 = _rows8(jnp.where(blk_i < n_valid_lt, gate, NEG_INF))
    blk3 = _rows8(blk)
    bias3 = jnp.full(cur3.shape, NEG_INF, F32)
    for _ in range(n_sel):
        mx = _allreduce8(jnp.max(cur3, axis=0), jnp.maximum)
        first = _allreduce8(jnp.min(jnp.where(cur3 == mx[None], blk3, float(LANES)), axis=0),
                            jnp.minimum)
        pick = jnp.logical_and(mx[None] > NEG_INF, blk3 == first[None])
        bias3 = jnp.where(pick, 0.0, bias3)
        cur3 = jnp.where(pick, NEG_INF, cur3)
    return bias3.reshape(gate.shape)


def _moba_prompt_kernel(qt_ref, kmean_ref, k_ref, vt_ref, o_ref, selb_sc, qg_sc, m_sc, acc_sc,
                        *, tq, n_sel):
    qi = pl.program_id(1)
    tk = tq
    width = CHAIN * tq
    n_blocks = selb_sc.shape[1]
    km = kmean_ref[...]
    _group_queries(qg_sc, lambda h: qt_ref[h * LANES:(h + 1) * LANES, :], tq)
    for c in range(N_CHAINS):
        bias = _top_blocks_bias_t(_dot(km, qg_sc[c]), qi, n_sel)
        for b in range(n_blocks):
            selb_sc[c, b] = jnp.broadcast_to(bias[b:b + 1, :], (SUBLANES, width))
    _init_state_t(m_sc, acc_sc)

    def tile(ki, causal):
        kt = k_ref[pl.ds(pl.multiple_of(ki * tk, tk), tk), :]
        for c in range(N_CHAINS):
            s = _dot(kt, qg_sc[c])
            if causal is not None:
                s = s + causal
            else:
                s = (_rows8(s) + selb_sc[c, ki][None]).reshape(tk, width)
            _update_t(c, s, vt_ref.at[ki], m_sc, acc_sc)

    tile(qi, _per_head(_causal_bias_t(tk, tq)))
    _for_tiles(qi, lambda ki: tile(ki, None))
    _write_t(o_ref, acc_sc, tq)


def _moba_prompt_call(qbt, kmean, kvb, vt, batch, t_len):
    n = kvb.shape[0]
    tq = MOBA_BLOCK
    nq = t_len // tq
    n_sel = min(MOBA_TOPK, nq)
    colblk = lambda rows: pl.BlockSpec((rows, tq), lambda b, i: (0, b * nq + i))
    return pl.pallas_call(
        functools.partial(_moba_prompt_kernel, tq=tq, n_sel=n_sel),
        grid=(batch, nq),
        in_specs=[colblk(N_HEADS * LANES),
                  pl.BlockSpec((None, LANES, LANES), lambda b, i: (b, 0, 0)),
                  pl.BlockSpec((t_len, LANES), lambda b, i: (b, 0)),
                  _vt_spec(nq, tq)],
        out_specs=colblk(BRANCH_W),
        out_shape=jax.ShapeDtypeStruct((BRANCH_W, n), BF16),
        scratch_shapes=[pltpu.VMEM((N_CHAINS, nq, SUBLANES, CHAIN * tq), F32)] + _attn_scratch(tq, LANES),
        compiler_params=_prompt_params(),
        name="moba_prompt",
    )(qbt, kmean, kvb, vt)


def _merge_kernel(x_ref, oa_ref, ob_ref, oc_ref, sz_ref, gate_ref, wbr_ref, wout_ref, gfin_ref,
                  o_ref, *, final):
    m = None
    for n, o_n in enumerate((oa_ref, ob_ref, oc_ref)):
        a = o_n[...] * sz_ref[:, n * BRANCH_W:(n + 1) * BRANCH_W]
        u = _dot(a, wbr_ref[n])
        t = gate_ref[:, n * D_MODEL:(n + 1) * D_MODEL].astype(F32) * u
        m = t if m is None else m + t
    y = x_ref[...] + _dot(m.astype(BF16), wout_ref[...])
    if final:
        ms = jnp.mean(y * y, axis=-1, keepdims=True)
        y = y * lax.rsqrt(ms + RMS_EPS) * gfin_ref[...]
    o_ref[...] = y


def _merge_call(x, oa, ob, oc, sz, gates, wbr, wout, gfin, tm, final):
    n = x.shape[0]
    row = lambda width: pl.BlockSpec((tm, width), lambda i: (i, 0))
    return pl.pallas_call(
        functools.partial(_merge_kernel, final=final),
        grid=(n // tm,),
        in_specs=[row(D_MODEL), row(BRANCH_W), row(BRANCH_W), row(BRANCH_W),
                  row(N_BRANCH * BRANCH_W), row(N_BRANCH * D_MODEL),
                  pl.BlockSpec((N_BRANCH, BRANCH_W, D_MODEL), lambda i: (0, 0, 0)),
                  pl.BlockSpec((D_MODEL, D_MODEL), lambda i: (0, 0)),
                  pl.BlockSpec((1, D_MODEL), lambda i: (0, 0))],
        out_specs=row(D_MODEL),
        out_shape=jax.ShapeDtypeStruct((n, D_MODEL), F32),
        compiler_params=pltpu.CompilerParams(dimension_semantics=("arbitrary",),
                                             vmem_limit_bytes=VMEM_LIMIT),
        name="merge",
    )(x, oa, ob, oc, sz, gates, wbr, wout, gfin)


def _expand8(x):
    return jnp.concatenate([jnp.broadcast_to(x[h:h + 1, :], (SUBLANES, x.shape[1]))
                            for h in range(x.shape[0])], axis=0)


def _tile8(x, reps):
    return jnp.concatenate([x] * reps, axis=0)


def _new_key_bias(rows, n_new):
    t8 = lax.broadcasted_iota(jnp.int32, (rows, LANES), 0) % SUBLANES
    lane = lax.broadcasted_iota(jnp.int32, (rows, LANES), 1)
    return jnp.where(lane <= jnp.minimum(t8, n_new - 1), 0.0, NEG_INF).astype(F32)


def _init_state_d(m_sc, l_sc, acc_sc):
    m_sc[...] = jnp.full(m_sc.shape, NEG_INF, F32)
    l_sc[...] = jnp.zeros(l_sc.shape, F32)
    acc_sc[...] = jnp.zeros(acc_sc.shape, F32)


def _update_d(s, pv, m_sc, l_sc, acc_sc):
    m_prev = m_sc[...]
    m_new = jnp.maximum(m_prev, jnp.max(s, axis=1, keepdims=True))
    m_safe = jnp.where(m_new == NEG_INF, 0.0, m_new)
    p = jnp.exp2(s - m_safe)
    alpha = jnp.exp2(m_prev - m_safe)
    l_sc[...] = alpha * l_sc[...] + jnp.sum(p, axis=1, keepdims=True)
    acc_sc[...] = alpha * acc_sc[...] + pv(p.astype(BF16))
    m_sc[...] = m_new


def _page_logits(q, pages):
    return _dot(q, jnp.concatenate([pg[0].astype(BF16) for pg in pages], axis=1))


def _page_pv(pages):
    def pv(p):
        acc = None
        for j, pg in enumerate(pages):
            t = _dot_nt(p[:, j * LANES:(j + 1) * LANES], pg[1].astype(BF16))
            acc = t if acc is None else acc + t
        return acc
    return pv


def _decode_state():
    rows = N_HEADS * SUBLANES
    return [pltpu.VMEM((rows, 1), F32), pltpu.VMEM((rows, 1), F32), pltpu.VMEM((rows, LANES), F32)]


def _dsa_decode_kernel(pt_ref, qi_ref, w_ref, q_ref, knew_idx_ref, knew_ref, vnew_ref, *rest,
                       n_pg, n_groups, n_sel, n_new, nbits):
    idx_pages = rest[:n_pg]
    kv_pages = rest[n_pg:2 * n_pg]
    o_ref, s_sc, thr_sc, cut_sc, m_sc, l_sc, acc_sc = rest[2 * n_pg:]
    ph = pl.program_id(1)
    g = pl.program_id(2)
    width = n_pg * LANES
    lane_w = lax.broadcasted_iota(jnp.int32, (SUBLANES, width), 1)

    def scores(r):
        r = jnp.maximum(r, 0.0) * w_ref[:, 0:1]
        acc = r[0:SUBLANES]
        for h in range(1, IDX_HEADS):
            acc = acc + r[h * SUBLANES:(h + 1) * SUBLANES]
        return _sortable_key(acc)

    @pl.when(ph == 0)
    def _():
        qi = qi_ref[:, 0:IDX_DIM]
        r = _dot(qi, jnp.concatenate([pg[...].astype(BF16) for pg in idx_pages], axis=1))
        s_sc[g] = scores(r)

    @pl.when(jnp.logical_and(ph == 0, g == n_groups - 1))
    def _():
        key = scores(_dot_nt(qi_ref[...], knew_idx_ref[...]))
        visible = _new_key_bias(SUBLANES, n_new) == 0.0
        t8 = lax.broadcasted_iota(jnp.int32, (SUBLANES, LANES), 0)
        key = jnp.where(jnp.logical_and(visible, t8 < n_new), key, INT_MIN)
        s_sc[n_groups] = jnp.full((SUBLANES, width), INT_MIN, jnp.int32)
        s_sc[n_groups, :, 0:LANES] = key

        def counter(pred):
            def count(arg):
                acc = jnp.zeros((SUBLANES, width), F32)
                for gi in range(n_groups + 1):
                    acc = acc + jnp.where(pred(s_sc[gi], gi * width + lane_w, arg), 1.0, 0.0)
                return jnp.sum(acc, axis=1, keepdims=True)
            return count

        kf = float(n_sel)
        stat = (SUBLANES, 1)
        thr, _ = _kth_largest(counter(lambda key, idx, x: key >= x), stat, kf)
        need = kf - counter(lambda key, idx, x: key > x)(thr)
        tie_lt = counter(lambda key, idx, x: jnp.logical_and(key == thr, idx < x))
        cut = _tie_cutoff(tie_lt, need, stat, nbits)
        thr_sc[...] = jnp.broadcast_to(thr, (SUBLANES, LANES))
        cut_sc[...] = jnp.broadcast_to(cut, (SUBLANES, LANES))
        _init_state_d(m_sc, l_sc, acc_sc)

    def sel_bias(key, idx):
        sel = _selected(key, idx, thr_sc[:, 0:1], cut_sc[:, 0:1])
        return _tile8(jnp.where(sel, 0.0, NEG_INF).astype(F32), N_HEADS)

    @pl.when(ph == 1)
    def _():
        s = _page_logits(q_ref[...], kv_pages) + sel_bias(s_sc[g], g * width + lane_w)
        _update_d(s, _page_pv(kv_pages), m_sc, l_sc, acc_sc)

    @pl.when(jnp.logical_and(ph == 1, g == n_groups - 1))
    def _():
        key = s_sc[n_groups, :, 0:LANES]
        s = _dot_nt(q_ref[...], knew_ref[...]) + sel_bias(key, n_groups * width + lane_w[:, 0:LANES])
        _update_d(s, lambda p: _dot(p, vnew_ref[...]), m_sc, l_sc, acc_sc)
        o_ref[...] = acc_sc[...] / l_sc[...]


def _page_specs(n_pg, block, index_of):
    return [pl.BlockSpec(block, functools.partial(index_of, j)) for j in range(n_pg)]


KV_PAGE = (None, 2, 2 * HEAD_DIM, LANES)


def _dsa_decode_call(page_table, layer_off, qi_rows, w_rows, q_rows, knew_idx, knew, vnew,
                     cache_idx, cache_kv, n_pg):
    nb, n_pages = page_table.shape
    n_groups = n_pages // n_pg
    n_new = 4
    n_sel = min(DSA_TOPK, (n_pages * LANES + n_new) // 4)
    nbits = max(1, ((n_groups + 1) * n_pg * LANES - 1).bit_length())

    def idx_map(j, b, ph, g, pt):
        gg = jnp.where(ph == 0, g, n_groups - 1)
        return (pt[b, gg * n_pg + j] + layer_off, 0, 0)

    def kv_map(j, b, ph, g, pt):
        gg = jnp.where(ph == 1, g, 0)
        return (pt[b, gg * n_pg + j] + layer_off, 0, 0, 0)

    per_b = lambda r, c: pl.BlockSpec((None, r, c), lambda b, ph, g, pt: (b, 0, 0))
    rows_q = N_HEADS * SUBLANES
    rows_i = IDX_HEADS * SUBLANES
    grid_spec = pltpu.PrefetchScalarGridSpec(
        num_scalar_prefetch=1,
        grid=(nb, 2, n_groups),
        in_specs=[per_b(rows_i, LANES), per_b(rows_i, LANES),
                  per_b(rows_q, LANES), per_b(LANES, LANES), per_b(LANES, LANES), per_b(LANES, LANES)]
        + _page_specs(n_pg, (None, IDX_DIM, LANES), idx_map)
        + _page_specs(n_pg, KV_PAGE, kv_map),
        out_specs=per_b(rows_q, LANES),
        scratch_shapes=[pltpu.VMEM((n_groups + 1, SUBLANES, n_pg * LANES), jnp.int32),
                        pltpu.VMEM((SUBLANES, LANES), jnp.int32), pltpu.VMEM((SUBLANES, LANES), jnp.int32)]
        + _decode_state(),
    )
    return pl.pallas_call(
        functools.partial(_dsa_decode_kernel, n_pg=n_pg, n_groups=n_groups, n_sel=n_sel,
                          n_new=n_new, nbits=nbits),
        grid_spec=grid_spec,
        out_shape=jax.ShapeDtypeStruct((nb, rows_q, LANES), F32),
        compiler_params=pltpu.CompilerParams(
            dimension_semantics=("arbitrary", "arbitrary", "arbitrary"), vmem_limit_bytes=VMEM_LIMIT),
        name="dsa_decode",
    )(page_table, qi_rows, w_rows, q_rows, knew_idx, knew, vnew,
      *([cache_idx] * n_pg), *([cache_kv] * n_pg))


def _moba_decode_kernel(pt_ref, q_ref, knew_ref, vnew_ref, *rest, n_pg, n_groups, n_sel, n_new):
    kv_pages = rest[:n_pg]
    o_ref, gate_sc, mb_sc, lb_sc, accb_sc, m_sc, l_sc, acc_sc = rest[n_pg:]
    g = pl.program_id(1)
    ppb = MOBA_BLOCK // LANES
    blocks_per_step = n_pg // ppb
    n_blocks = n_groups * blocks_per_step
    rows = N_HEADS * SUBLANES
    q = q_ref[...]

    for jb in range(blocks_per_step):
        pages = kv_pages[jb * ppb:(jb + 1) * ppb]
        ksum = None
        for pg in pages:
            t = jnp.sum(pg[0], axis=1, keepdims=True)
            ksum = t if ksum is None else ksum + t
        kmean = jnp.broadcast_to(ksum * (1.0 / MOBA_BLOCK), (LANES, LANES)).astype(BF16)
        gate = _dot(q, kmean)
        s = _page_logits(q, pages)
        mj = jnp.max(s, axis=1, keepdims=True)
        p = jnp.exp2(s - mj)
        b = g * blocks_per_step + jb
        gate_sc[b] = gate
        mb_sc[b] = jnp.broadcast_to(mj, (rows, LANES))
        lb_sc[b] = jnp.broadcast_to(jnp.sum(p, axis=1, keepdims=True), (rows, LANES))
        accb_sc[b] = _page_pv(pages)(p.astype(BF16))

    @pl.when(g == n_groups - 1)
    def _():
        _init_state_d(m_sc, l_sc, acc_sc)
        s = _dot_nt(q, knew_ref[...]) + _new_key_bias(rows, n_new)
        _update_d(s, lambda p: _dot(p, vnew_ref[...]), m_sc, l_sc, acc_sc)
        m_own = jnp.broadcast_to(m_sc[...], (rows, LANES))

        def pick_round(_, c):
            def mx_body(b, mx):
                return jnp.maximum(mx, gate_sc[b])
            mx = lax.fori_loop(0, n_blocks, mx_body, jnp.full((rows, LANES), NEG_INF, F32))

            def first_body(b, first):
                return jnp.minimum(first, jnp.where(gate_sc[b] == mx, b, n_blocks))
            first = lax.fori_loop(0, n_blocks, first_body, jnp.full((rows, LANES), n_blocks, jnp.int32))

            def mark_body(b, cc):
                hit = jnp.logical_and(first == b, mx > NEG_INF)
                gate_sc[b] = jnp.where(hit, NEG_INF, gate_sc[b])
                lb_sc[b] = jnp.where(hit, -lb_sc[b], lb_sc[b])
                return cc
            lax.fori_loop(0, n_blocks, mark_body, 0)
            return c

        lax.fori_loop(0, n_sel, pick_round, 0)

        def max_body(b, mt):
            return jnp.maximum(mt, jnp.where(lb_sc[b] < 0.0, mb_sc[b], NEG_INF))
        m_tot = lax.fori_loop(0, n_blocks, max_body, m_own)

        def comb_body(b, carry):
            l_tot, acc_tot = carry
            picked = lb_sc[b] < 0.0
            wgt = jnp.where(picked, jnp.exp2(mb_sc[b] - m_tot), 0.0)
            return l_tot - wgt * lb_sc[b], acc_tot + wgt * accb_sc[b]

        w_own = jnp.exp2(m_own - m_tot)
        l0 = w_own * jnp.broadcast_to(l_sc[...], (rows, LANES))
        a0 = w_own * acc_sc[...]
        l_tot, acc_tot = lax.fori_loop(0, n_blocks, comb_body, (l0, a0))
        o_ref[...] = acc_tot / l_tot


def _moba_decode_call(page_table, layer_off, q_rows, knew, vnew, cache_kv, n_pg):
    nb, n_pages = page_table.shape
    n_groups = n_pages // n_pg
    n_blocks = n_pages * LANES // MOBA_BLOCK
    n_sel = min(MOBA_TOPK, n_blocks + 1)
    rows_q = N_HEADS * SUBLANES

    def kv_map(j, b, g, pt):
        return (pt[b, g * n_pg + j] + layer_off, 0, 0, 0)

    per_b = lambda r, c: pl.BlockSpec((None, r, c), lambda b, g, pt: (b, 0, 0))
    slab = lambda: pltpu.VMEM((n_blocks, rows_q, LANES), F32)
    grid_spec = pltpu.PrefetchScalarGridSpec(
        num_scalar_prefetch=1,
        grid=(nb, n_groups),
        in_specs=[per_b(rows_q, LANES), per_b(LANES, LANES), per_b(LANES, LANES)]
        + _page_specs(n_pg, KV_PAGE, kv_map),
        out_specs=per_b(rows_q, LANES),
        scratch_shapes=[slab(), slab(), slab(), slab()] + _decode_state(),
    )
    return pl.pallas_call(
        functools.partial(_moba_decode_kernel, n_pg=n_pg, n_groups=n_groups, n_sel=n_sel, n_new=4),
        grid_spec=grid_spec,
        out_shape=jax.ShapeDtypeStruct((nb, rows_q, LANES), F32),
        compiler_params=pltpu.CompilerParams(dimension_semantics=("arbitrary", "arbitrary"),
                                             vmem_limit_bytes=VMEM_LIMIT),
        name="moba_decode",
    )(page_table, q_rows, knew, vnew, *([cache_kv] * n_pg))


def _fox_decode_kernel(pt_ref, q_ref, knew_ref, vnew_ref, lfnew_rows_ref, lfnew_t_ref,
                       tri_ref, upper_ref, sfx_ref, *rest, n_pg, n_groups, n_new):
    kv_pages = rest[:n_pg]
    lf_pages = rest[n_pg:2 * n_pg]
    o_ref, cq_sc, carry_sc, m_sc, l_sc, acc_sc = rest[2 * n_pg:]
    g = pl.program_id(1)
    rows = N_HEADS * SUBLANES
    q = q_ref[...]

    @pl.when(g == 0)
    def _():
        cq_rows = _dot3_left(tri_ref[...], lfnew_rows_ref[...]) * LOG2E
        cq_sc[...] = cq_rows
        new_cum_t = _dot3(lfnew_t_ref[...], upper_ref[...]) * LOG2E
        carry_sc[...] = jnp.zeros(carry_sc.shape, F32)
        _init_state_d(m_sc, l_sc, acc_sc)
        s = (_dot_nt(q, knew_ref[...]) + cq_rows - _expand8(new_cum_t)
             + _new_key_bias(rows, n_new))
        _update_d(s, lambda p: _dot(p, vnew_ref[...]), m_sc, l_sc, acc_sc)

    carry = carry_sc[...]
    suffix = [None] * n_pg
    for j in range(n_pg - 1, -1, -1):
        lft = lf_pages[j][...]
        suffix[j] = (_dot3(lft, sfx_ref[...]) + carry) * LOG2E
        carry = carry + jnp.sum(lft, axis=1, keepdims=True)
    carry_sc[...] = carry
    bias = jnp.concatenate([_expand8(sf) for sf in suffix], axis=1)
    s = _page_logits(q, kv_pages) + bias + jnp.concatenate([cq_sc[...]] * n_pg, axis=1)
    _update_d(s, _page_pv(kv_pages), m_sc, l_sc, acc_sc)

    @pl.when(g == n_groups - 1)
    def _():
        o_ref[...] = acc_sc[...] / l_sc[...]


def _fox_decode_call(page_table, layer_off, q_rows, knew, vnew, lfnew_rows, lfnew_t,
                     cache_kv, cache_lft, n_pg):
    nb, n_pages = page_table.shape
    n_groups = n_pages // n_pg
    rows_q = N_HEADS * SUBLANES
    r = np.arange(rows_q)
    tri = ((r[:, None] // SUBLANES == r[None, :] // SUBLANES)
           & (r[None, :] % SUBLANES <= r[:, None] % SUBLANES)).astype(np.float32)
    lane = np.arange(LANES)
    upper = (lane[:, None] <= lane[None, :]).astype(np.float32)
    sfx = (lane[:, None] > lane[None, :]).astype(np.float32)

    def kv_map(j, b, g, pt):
        return (pt[b, (n_groups - 1 - g) * n_pg + j] + layer_off, 0, 0, 0)

    def lf_map(j, b, g, pt):
        return (pt[b, (n_groups - 1 - g) * n_pg + j] + layer_off, 0, 0)

    per_b = lambda rr, c: pl.BlockSpec((None, rr, c), lambda b, g, pt: (b, 0, 0))
    const = lambda rr, c: pl.BlockSpec((rr, c), lambda b, g, pt: (0, 0))
    grid_spec = pltpu.PrefetchScalarGridSpec(
        num_scalar_prefetch=1,
        grid=(nb, n_groups),
        in_specs=[per_b(rows_q, LANES), per_b(LANES, LANES), per_b(LANES, LANES),
                  per_b(rows_q, LANES), per_b(SUBLANES, LANES),
                  const(rows_q, rows_q), const(LANES, LANES), const(LANES, LANES)]
        + _page_specs(n_pg, KV_PAGE, kv_map)
        + _page_specs(n_pg, (None, SUBLANES, LANES), lf_map),
        out_specs=per_b(rows_q, LANES),
        scratch_shapes=[pltpu.VMEM((rows_q, LANES), F32), pltpu.VMEM((SUBLANES, LANES), F32)]
        + _decode_state(),
    )
    return pl.pallas_call(
        functools.partial(_fox_decode_kernel, n_pg=n_pg, n_groups=n_groups, n_new=4),
        grid_spec=grid_spec,
        out_shape=jax.ShapeDtypeStruct((nb, rows_q, LANES), F32),
        compiler_params=pltpu.CompilerParams(dimension_semantics=("arbitrary", "arbitrary"),
                                             vmem_limit_bytes=VMEM_LIMIT),
        name="fox_decode",
    )(page_table, q_rows, knew, vnew, lfnew_rows, lfnew_t,
      jnp.asarray(tri, BF16), jnp.asarray(upper, BF16), jnp.asarray(sfx, BF16),
      *([cache_kv] * n_pg), *([cache_lft] * n_pg))


def _rope_tables(pos):
    half = HEAD_DIM // 2
    freqs = ROPE_THETA ** (-jnp.arange(half, dtype=F32) / half)
    ang = pos.astype(F32)[:, None] * freqs[None, :]
    cos = jnp.tile(jnp.cos(ang), (1, LANES // half))
    sin = jnp.sin(ang)
    sin = jnp.tile(jnp.concatenate([-sin, sin], axis=1), (1, LANES // HEAD_DIM))
    return cos, sin


def _layout_w_in(w_in_l):
    cols = jnp.asarray(np.maximum(W_SRC, 0), jnp.int32)
    keep = jnp.asarray(W_SRC >= 0)
    return jnp.where(keep[None, :], jnp.take(w_in_l, cols, axis=1), 0.0).astype(BF16)


def _decode_rows(a, nb, t_new, heads):
    a = a.reshape(nb, t_new, heads, LANES).transpose(0, 2, 1, 3)
    a = jnp.pad(a, ((0, 0), (0, 0), (0, SUBLANES - t_new), (0, 0)))
    return a.reshape(nb, heads * SUBLANES, LANES)


def _new_rows(a, nb, t_new):
    a = a.reshape(nb, t_new, LANES)
    return jnp.pad(a, ((0, 0), (0, LANES - t_new), (0, 0)))


def _pick_pages_per_step(n_pages):
    for c in (32, 16, 8, 4, 2):
        if n_pages % c == 0:
            return c
    raise ValueError("page count must be even")


def _value_tiles(kv, tk):
    n = kv.shape[0]
    v = kv[:, KV_W:2 * KV_W].reshape(n // tk, tk, N_KV_HEADS, HEAD_DIM).transpose(0, 2, 3, 1)
    tail = jnp.zeros((n // tk, N_KV_HEADS, VT_ROWS - HEAD_DIM, tk), BF16).at[:, :, 0, :].set(1.0)
    return jnp.concatenate([v, tail], axis=2)


def _prompt_branches(proj, batch, t_len):
    (qa, qb, qi, qc, kva, kvb, kvc, kidx, _, _, bkv, _, _, misc, _, _) = proj
    tq = MOBA_BLOCK
    n = qa.shape[0]
    n_blocks = t_len // MOBA_BLOCK
    o_a = _dsa_prompt_call(qa.T, qi.T, misc.T, kidx, kva, _value_tiles(kva, tq), batch, t_len, tq)
    kmean = _kmean_call(bkv, batch * n_blocks).reshape(batch, n_blocks, LANES)
    kmean = jnp.pad(kmean, ((0, 0), (0, LANES - n_blocks), (0, 0))).astype(BF16)
    o_b = _moba_prompt_call(qb.T, kmean, kvb, _value_tiles(kvb, tq), batch, t_len)
    kaug, qaug = _foxprep_call(misc, kvc, batch, t_len, LANES)
    qpt = jnp.concatenate([qc.T.reshape(N_HEADS, LANES, n), qaug.T.reshape(N_HEADS, LANES, n)], axis=1)
    o_c = _fox_prompt_call(qpt, kaug, _value_tiles(kvc, tq), batch, t_len, tq)
    return o_a.T, o_b.T, o_c.T


def _sample_branches(proj, nb, t_new, page_table, layer_off, caches):
    (qa, qb, qi, qc, kva, kvb, kvc, kidx, _, _, _, _, logf, misc, _, _) = proj
    ca_idx, ca_kv, cb_kv, cc_kv, cc_lft = caches
    n_pg = _pick_pages_per_step(page_table.shape[1])

    def head_rows(col, heads):
        a = col.reshape(nb, t_new, heads).transpose(0, 2, 1)
        a = jnp.pad(a, ((0, 0), (0, 0), (0, SUBLANES - t_new)))
        return jnp.broadcast_to(a.reshape(nb, heads * SUBLANES, 1), (nb, heads * SUBLANES, LANES))

    def new_kv(kv):
        return _new_rows(kv[:, 0:LANES], nb, t_new), _new_rows(kv[:, LANES:2 * LANES], nb, t_new)

    d_a = _dsa_decode_call(
        page_table, layer_off, _decode_rows(qi, nb, t_new, IDX_HEADS),
        head_rows(misc[:, MISC_W:MISC_W + IDX_HEADS], IDX_HEADS),
        _decode_rows(qa, nb, t_new, N_HEADS), _new_rows(kidx, nb, t_new), *new_kv(kva),
        ca_idx, ca_kv, n_pg)
    d_b = _moba_decode_call(page_table, layer_off, _decode_rows(qb, nb, t_new, N_HEADS),
                            *new_kv(kvb), cb_kv, n_pg)
    lf_t = jnp.pad(logf.reshape(nb, t_new, N_HEADS).transpose(0, 2, 1),
                   ((0, 0), (0, 0), (0, LANES - t_new)))
    d_c = _fox_decode_call(page_table, layer_off, _decode_rows(qc, nb, t_new, N_HEADS),
                           *new_kv(kvc), head_rows(logf, N_HEADS), lf_t, cc_kv, cc_lft, n_pg)

    def compact(d):
        d = d.reshape(nb, N_KV_HEADS, GROUP, SUBLANES, N_KV_HEADS, HEAD_DIM)[:, :, :, :t_new]
        d = jnp.stack([d[:, gidx, :, :, gidx, :] for gidx in range(N_KV_HEADS)], axis=1)
        return d.transpose(0, 3, 1, 2, 4).reshape(nb * t_new, BRANCH_W).astype(BF16)

    return compact(d_a), compact(d_b), compact(d_c)


def _native_pages(cache, depth, n_pool):
    nd = cache.ndim
    c = jnp.transpose(cache, (0, 1) + tuple(range(3, nd)) + (2,))
    return c.reshape((depth * n_pool,) + c.shape[2:])


def kernel(x_prompt, x_sample, cache_a_kv, cache_a_idx, cache_b_kv, cache_c_kv, cache_c_logf,
           page_table, norm_g, w_in, b_forget, w_branch, w_out, final_norm_g):
    batch, t_len, _ = x_prompt.shape
    nb, t_new, _ = x_sample.shape
    depth = norm_g.shape[0]
    n_pool, page_size = cache_a_kv.shape[1], cache_a_kv.shape[2]
    n_pages = page_table.shape[1]
    past_len = n_pages * page_size
    assert page_size == LANES and t_len % MOBA_BLOCK == 0 and past_len % MOBA_BLOCK == 0 and t_new == 4
    n_p = batch * t_len
    n_s = nb * t_new

    pos_p = jnp.tile(jnp.arange(t_len, dtype=jnp.int32), batch)
    pos_s = jnp.tile(past_len + jnp.arange(t_new, dtype=jnp.int32), nb)
    cos_p, sin_p = _rope_tables(pos_p)
    cos_s, sin_s = _rope_tables(pos_s)

    kv_pages = lambda c: _native_pages(c, depth, n_pool).reshape(
        depth * n_pool, 2, N_KV_HEADS * HEAD_DIM, page_size)
    ca_kv, cb_kv, cc_kv = kv_pages(cache_a_kv), kv_pages(cache_b_kv), kv_pages(cache_c_kv)
    ca_idx = _native_pages(cache_a_idx, depth, n_pool)
    cc_lft = _native_pages(cache_c_logf, depth, n_pool)

    xp = x_prompt.reshape(n_p, D_MODEL)
    xs = x_sample.reshape(n_s, D_MODEL)
    gfin = final_norm_g.reshape(1, D_MODEL)
    rows_p = []
    rows_s = []
    for l in range(depth):
        w = _layout_w_in(w_in[l])
        g = norm_g[l].reshape(1, D_MODEL)
        bf_pad = jnp.zeros((1, LANES), F32).at[0, MISC_F:MISC_F + N_HEADS].set(b_forget[l])
        wbr = w_branch[l].astype(BF16)
        wout = w_out[l].astype(BF16)
        final = l == depth - 1

        proj_p = _proj_call(xp, g, w, cos_p, sin_p, bf_pad, 256)
        rows_p.append(proj_p[8:13])
        o_a, o_b, o_c = _prompt_branches(proj_p, batch, t_len)
        xp = _merge_call(xp, o_a, o_b, o_c, proj_p[14], proj_p[15], wbr, wout, gfin, 256, final)

        proj_s = _proj_call(xs, g, w, cos_s, sin_s, bf_pad, n_s)
        rows_s.append(proj_s[8:13])
        o_a, o_b, o_c = _sample_branches(proj_s, nb, t_new, page_table, l * n_pool,
                                         (ca_idx, ca_kv, cb_kv, cc_kv, cc_lft))
        xs = _merge_call(xs, o_a, o_b, o_c, proj_s[14], proj_s[15], wbr, wout, gfin, n_s, final)

    def stack(rows, i, b, t, tail):
        return jnp.stack([r[i] for r in rows]).reshape((depth, b, t) + tail)

    kv_tail = (2, N_KV_HEADS, HEAD_DIM)
    outs_p = (stack(rows_p, 0, batch, t_len, kv_tail), stack(rows_p, 1, batch, t_len, (IDX_DIM,)),
              stack(rows_p, 2, batch, t_len, kv_tail), stack(rows_p, 3, batch, t_len, kv_tail),
              stack(rows_p, 4, batch, t_len, (N_HEADS,)))
    outs_s = (stack(rows_s, 0, nb, t_new, kv_tail), stack(rows_s, 1, nb, t_new, (IDX_DIM,)),
              stack(rows_s, 2, nb, t_new, kv_tail), stack(rows_s, 3, nb, t_new, kv_tail),
              stack(rows_s, 4, nb, t_new, (N_HEADS,)))
    return (xp.reshape(batch, t_len, D_MODEL), xs.reshape(nb, t_new, D_MODEL)) + outs_p + outs_s
```

```python
import functools

import numpy as np
import jax
import jax.numpy as jnp
from jax import lax
from jax.experimental import pallas as pl
from jax.experimental.pallas import tpu as pltpu

D_MODEL = 1024
HEAD_DIM = 64
N_HEADS = 8
N_KV_HEADS = 2
GROUP = N_HEADS // N_KV_HEADS
BRANCH_W = N_HEADS * HEAD_DIM
KV_W = N_KV_HEADS * HEAD_DIM
N_BRANCH = 3
IDX_HEADS = 4
IDX_DIM = 64
DSA_TOPK = 256
MOBA_BLOCK = 256
MOBA_TOPK = 3
ROPE_THETA = 10000.0
RMS_EPS = 1e-6
ATTN_SCALE = HEAD_DIM ** -0.5
IDX_SCALE = (IDX_HEADS ** -0.5) * (IDX_DIM ** -0.5)
LOG2E = float(np.log2(np.e))
QK_SCALE = ATTN_SCALE * LOG2E

LANES = 128
SUBLANES = 8
VMEM_LIMIT = 56 * 1024 * 1024
INT_MIN = -2 ** 31
NEG_INF = float("-inf")
F32 = jnp.float32
BF16 = jnp.bfloat16

VT_ROWS = 80
VT_ONE = HEAD_DIM

IN_SPLITS = (
    ('a_q', BRANCH_W), ('a_k', KV_W), ('a_v', KV_W), ('a_z', BRANCH_W),
    ('a_iq', IDX_HEADS * IDX_DIM), ('a_ik', IDX_DIM), ('a_iw', IDX_HEADS),
    ('b_q', BRANCH_W), ('b_k', KV_W), ('b_v', KV_W), ('b_z', BRANCH_W),
    ('c_q', BRANCH_W), ('c_k', KV_W), ('c_v', KV_W), ('c_z', BRANCH_W), ('c_f', N_HEADS),
    ('gates', N_BRANCH * D_MODEL),
)
IN_OFF = {}
_o = 0
for _n, _w in IN_SPLITS:
    IN_OFF[_n] = _o
    _o += _w
IN_WIDTH = _o

MISC_F = 0
MISC_W = 8
AUG_ONE = 0
AUG_C = 3


def _wide_q_cols(name):
    src = np.full((N_HEADS * LANES,), -1, np.int64)
    for h in range(N_HEADS):
        g = h // GROUP
        dst = h * LANES + g * HEAD_DIM
        src[dst:dst + HEAD_DIM] = IN_OFF[name] + h * HEAD_DIM + np.arange(HEAD_DIM)
    return src


def _seg(name, width, pad_to=None):
    src = IN_OFF[name] + np.arange(width)
    if pad_to is not None and pad_to > width:
        src = np.concatenate([src, np.full((pad_to - width,), -1, np.int64)])
    return src


def _build_layout():
    segs = []
    segs.append(('a_qw', _wide_q_cols('a_q')))
    segs.append(('b_qw', _wide_q_cols('b_q')))
    iq = np.full((IDX_HEADS * LANES,), -1, np.int64)
    for h in range(IDX_HEADS):
        iq[h * LANES:h * LANES + IDX_DIM] = IN_OFF['a_iq'] + h * IDX_DIM + np.arange(IDX_DIM)
    segs.append(('a_iqw', iq))
    segs.append(('a_k', _seg('a_k', KV_W)))
    segs.append(('b_k', _seg('b_k', KV_W)))
    segs.append(('a_ik', _seg('a_ik', IDX_DIM, LANES)))
    segs.append(('a_v', _seg('a_v', KV_W)))
    segs.append(('b_v', _seg('b_v', KV_W)))
    segs.append(('c_qw', _wide_q_cols('c_q')))
    segs.append(('c_k', _seg('c_k', KV_W)))
    segs.append(('c_v', _seg('c_v', KV_W)))
    segs.append(('a_z', _seg('a_z', BRANCH_W)))
    segs.append(('b_z', _seg('b_z', BRANCH_W)))
    segs.append(('c_z', _seg('c_z', BRANCH_W)))
    misc = np.full((LANES,), -1, np.int64)
    misc[MISC_F:MISC_F + N_HEADS] = IN_OFF['c_f'] + np.arange(N_HEADS)
    misc[MISC_W:MISC_W + IDX_HEADS] = IN_OFF['a_iw'] + np.arange(IDX_HEADS)
    segs.append(('misc', misc))
    segs.append(('gates', _seg('gates', N_BRANCH * D_MODEL)))
    off = {}
    o = 0
    for n, s in segs:
        off[n] = (o, len(s))
        o += len(s)
    return np.concatenate([s for _, s in segs]), off, o


W_SRC, W_OFF, W_TOTAL = _build_layout()


def _dot(a, b):
    return jnp.dot(a, b, preferred_element_type=F32)


def _dot_nt(a, b):
    return lax.dot_general(a, b, (((1,), (1,)), ((), ())), preferred_element_type=F32)


def _split3(x):
    hi = x.astype(BF16)
    r1 = x - hi.astype(F32)
    mid = r1.astype(BF16)
    lo = (r1 - mid.astype(F32)).astype(BF16)
    return hi, mid, lo


def _dot3(x, m_bf16):
    hi, mid, lo = _split3(x)
    return _dot(hi, m_bf16) + _dot(mid, m_bf16) + _dot(lo, m_bf16)


def _dot3_left(m_bf16, x):
    hi, mid, lo = _split3(x)
    return _dot(m_bf16, hi) + _dot(m_bf16, mid) + _dot(m_bf16, lo)


def _proj_kernel(x_ref, g_ref, w_ref, cos_ref, sin_ref, bf_ref,
                 qa_ref, qb_ref, qi_ref, qc_ref, kva_ref, kvb_ref, kvc_ref, kidx_ref,
                 akv_ref, aidx_ref, bkv_ref, ckv_ref, logf_ref, misc_ref, sz_ref, gate_ref):
    x = x_ref[...]
    ms = jnp.mean(x * x, axis=-1, keepdims=True)
    h = (x * lax.rsqrt(ms + RMS_EPS) * g_ref[...]).astype(BF16)
    cos = cos_ref[...]
    sin = sin_ref[...]
    tm = x.shape[0]
    lane = lax.broadcasted_iota(jnp.int32, (tm, LANES), 1)
    first_half = (lane % HEAD_DIM) < (HEAD_DIM // 2)

    def proj(name):
        o, w = W_OFF[name]
        return _dot(h, w_ref[:, o:o + w])

    def rope(y):
        partner = jnp.where(first_half, pltpu.roll(y, LANES - HEAD_DIM // 2, 1),
                            pltpu.roll(y, HEAD_DIM // 2, 1))
        return y * cos + partner * sin

    for name, ref in (('a_qw', qa_ref), ('b_qw', qb_ref)):
        y = proj(name)
        for c in range(N_HEADS):
            ref[:, c * LANES:(c + 1) * LANES] = (
                rope(y[:, c * LANES:(c + 1) * LANES]) * QK_SCALE).astype(BF16)
    y = proj('a_iqw')
    for c in range(IDX_HEADS):
        qi_ref[:, c * LANES:(c + 1) * LANES] = rope(y[:, c * LANES:(c + 1) * LANES]).astype(BF16)
    y = proj('c_qw')
    qc_ref[...] = (y * QK_SCALE).astype(BF16)

    for kname, vname, f32_ref, bf_out in (('a_k', 'a_v', akv_ref, kva_ref),
                                          ('b_k', 'b_v', bkv_ref, kvb_ref),
                                          ('c_k', 'c_v', ckv_ref, kvc_ref)):
        k = proj(kname)
        if kname != 'c_k':
            k = rope(k)
        v = proj(vname)
        f32_ref[:, 0:LANES] = k
        f32_ref[:, LANES:2 * LANES] = v
        bf_out[:, 0:LANES] = k.astype(BF16)
        bf_out[:, LANES:2 * LANES] = v.astype(BF16)

    ik = rope(proj('a_ik'))
    aidx_ref[...] = ik[:, 0:IDX_DIM]
    kidx_ref[...] = ik.astype(BF16)

    for i, name in enumerate(('a_z', 'b_z', 'c_z')):
        z = proj(name)
        sz_ref[:, i * BRANCH_W:(i + 1) * BRANCH_W] = (z * jax.nn.sigmoid(z)).astype(BF16)

    y = proj('misc')
    t = y + bf_ref[...]
    logf = jnp.minimum(t, 0.0) - jnp.log1p(jnp.exp(-jnp.abs(t)))
    misc = jnp.where(lane < MISC_W, logf, y * IDX_SCALE)
    misc_ref[...] = misc
    logf_ref[...] = misc[:, MISC_F:MISC_F + N_HEADS]

    y = proj('gates')
    gate_ref[...] = jax.nn.sigmoid(y).astype(BF16)


def _proj_call(x, g, w, cos, sin, bf_pad, tm):
    n = x.shape[0]
    row = lambda width: pl.BlockSpec((tm, width), lambda i: (i, 0))
    const = lambda shape: pl.BlockSpec(shape, lambda i: (0, 0))
    out_shapes = [
        ((n, N_HEADS * LANES), BF16),
        ((n, N_HEADS * LANES), BF16),
        ((n, IDX_HEADS * LANES), BF16),
        ((n, N_HEADS * LANES), BF16),
        ((n, 2 * KV_W), BF16),
        ((n, 2 * KV_W), BF16),
        ((n, 2 * KV_W), BF16),
        ((n, LANES), BF16),
        ((n, 2 * KV_W), F32),
        ((n, IDX_DIM), F32),
        ((n, 2 * KV_W), F32),
        ((n, 2 * KV_W), F32),
        ((n, N_HEADS), F32),
        ((n, LANES), F32),
        ((n, N_BRANCH * BRANCH_W), BF16),
        ((n, N_BRANCH * D_MODEL), BF16),
    ]
    return pl.pallas_call(
        _proj_kernel,
        grid=(n // tm,),
        in_specs=[row(D_MODEL), const((1, D_MODEL)),
                  pl.BlockSpec((D_MODEL, W_TOTAL), lambda i: (0, 0), pipeline_mode=pl.Buffered(1)),
                  row(LANES), row(LANES), const((1, LANES))],
        out_specs=[row(s[1]) for s, _ in out_shapes],
        out_shape=[jax.ShapeDtypeStruct(s, d) for s, d in out_shapes],
        compiler_params=pltpu.CompilerParams(dimension_semantics=("arbitrary",),
                                             vmem_limit_bytes=VMEM_LIMIT),
        name="proj",
    )(x, g, w, cos, sin, bf_pad)


def _rows8(x):
    return x.reshape(x.shape[0] // SUBLANES, SUBLANES, x.shape[1])


def _allreduce8(x8, op):
    for shift in (4, 2, 1):
        x8 = op(x8, pltpu.roll(x8, shift, 0))
    return x8


def _init_state_t(m_sc, acc_sc):
    m_sc[...] = jnp.full(m_sc.shape, NEG_INF, F32)
    acc_sc[...] = jnp.zeros(acc_sc.shape, F32)


CHAIN = 8
N_CHAINS = N_HEADS // CHAIN


def _update_t(c, s, vt_tile, m_sc, acc_sc):
    s3 = _rows8(s)
    m_prev = m_sc[c]
    m_new = jnp.maximum(m_prev, _allreduce8(jnp.max(s3, axis=0), jnp.maximum))
    m_safe = jnp.where(m_new == NEG_INF, 0.0, m_new)
    p = jnp.exp2(s3 - m_safe[None]).reshape(s.shape).astype(BF16)
    alpha = jnp.exp2(m_prev - m_safe)
    acc = acc_sc[c]
    per_group = GROUP * (s.shape[1] // CHAIN)
    first = (c * CHAIN) // GROUP
    pv = [_dot(vt_tile[first + i], p[:, i * per_group:(i + 1) * per_group])
          for i in range(max(1, CHAIN // GROUP))]
    pv = pv[0] if len(pv) == 1 else jnp.concatenate(pv, axis=1)
    acc_sc[c] = (_rows8(acc) * alpha[None]).reshape(acc.shape) + pv
    m_sc[c] = m_new


def _group_queries(qg_sc, head_block, tq):
    for h in range(N_HEADS):
        c, j = divmod(h, CHAIN)
        qg_sc[c, :, j * tq:(j + 1) * tq] = head_block(h)


def _per_head(x):
    return jnp.concatenate([x] * CHAIN, axis=1)


def _write_t(o_ref, acc_sc, tq):
    for h in range(N_HEADS):
        c, j = divmod(h, CHAIN)
        acc = acc_sc[c, :, j * tq:(j + 1) * tq]
        o_ref[h * HEAD_DIM:(h + 1) * HEAD_DIM, :] = (
            acc[0:HEAD_DIM] / acc[VT_ONE:VT_ONE + 1]).astype(o_ref.dtype)


def _for_tiles(n, tile):
    def body(i, c):
        tile(2 * i)
        tile(2 * i + 1)
        return c

    lax.fori_loop(0, n // 2, body, 0)

    @pl.when(n % 2 == 1)
    def _():
        tile(n - 1)


def _causal_bias_t(tk, tq):
    key = lax.broadcasted_iota(jnp.int32, (tk, tq), 0)
    qry = lax.broadcasted_iota(jnp.int32, (tk, tq), 1)
    return jnp.where(key <= qry, 0.0, NEG_INF).astype(F32)


def _attn_scratch(tq, dk):
    return [pltpu.VMEM((N_CHAINS, dk, CHAIN * tq), BF16),
            pltpu.VMEM((N_CHAINS, SUBLANES, CHAIN * tq), F32),
            pltpu.VMEM((N_CHAINS, VT_ROWS, CHAIN * tq), F32)]


def _prompt_params():
    return pltpu.CompilerParams(dimension_semantics=("arbitrary", "arbitrary"),
                                vmem_limit_bytes=VMEM_LIMIT)


def _fox_aug_mats():
    pk = np.zeros((3, LANES, LANES), np.float32)
    pq = np.zeros((3, LANES, N_HEADS * LANES), np.float32)
    kconst = np.zeros((1, LANES), np.float32)
    qconst = np.zeros((1, N_HEADS * LANES), np.float32)
    for t in range(3):
        kconst[0, AUG_ONE + t] = 1.0
        for h in range(N_HEADS):
            pk[t, MISC_F + h, AUG_C + 3 * h + t] = -1.0
            pq[t, MISC_F + h, h * LANES + AUG_ONE + t] = 1.0
            qconst[0, h * LANES + AUG_C + 3 * h + t] = 1.0
    return pk, pq, kconst, qconst


def _foxprep_kernel(misc_ref, kv_ref, ltri_ref, pk_ref, pq_ref, kconst_ref, qconst_ref,
                    kaug_ref, qaug_ref, carry_sc):
    @pl.when(pl.program_id(1) == 0)
    def _():
        carry_sc[...] = jnp.zeros(carry_sc.shape, F32)

    lf = misc_ref[...]
    rows = lf.shape[0]
    c = _dot3_left(ltri_ref[...], lf) + carry_sc[0:1, :]
    carry_sc[0:1, :] = c[rows - 1:rows, :]
    ka = kconst_ref[...]
    qa = qconst_ref[...]
    for t, part in enumerate(_split3(c * LOG2E)):
        ka = ka + _dot(part, pk_ref[t])
        qa = qa + _dot(part, pq_ref[t])
    kaug_ref[:, 0:LANES] = kv_ref[...]
    kaug_ref[:, LANES:2 * LANES] = ka.astype(BF16)
    qaug_ref[...] = qa.astype(BF16)


def _foxprep_call(misc, kvc, batch, t_len, chunk):
    n = misc.shape[0]
    nch = t_len // chunk
    pk, pq, kconst, qconst = _fox_aug_mats()
    ltri = np.tril(np.ones((chunk, chunk), np.float32))
    rowblk = lambda width, col=0: pl.BlockSpec((chunk, width), lambda b, i: (b * nch + i, col))
    const = lambda shape: pl.BlockSpec(shape, lambda b, i: tuple(0 for _ in shape))
    return pl.pallas_call(
        _foxprep_kernel,
        grid=(batch, nch),
        in_specs=[rowblk(LANES), rowblk(LANES), const((chunk, chunk)), const((3, LANES, LANES)),
                  const((3, LANES, N_HEADS * LANES)), const((1, LANES)), const((1, N_HEADS * LANES))],
        out_specs=[rowblk(2 * LANES), rowblk(N_HEADS * LANES)],
        out_shape=[jax.ShapeDtypeStruct((n, 2 * LANES), BF16),
                   jax.ShapeDtypeStruct((n, N_HEADS * LANES), BF16)],
        scratch_shapes=[pltpu.VMEM((SUBLANES, LANES), F32)],
        compiler_params=_prompt_params(),
        name="foxprep",
    )(misc, kvc, jnp.asarray(ltri, BF16), jnp.asarray(pk, BF16), jnp.asarray(pq, BF16),
      jnp.asarray(kconst, F32), jnp.asarray(qconst, F32))


def _fox_prompt_kernel(qt_ref, kaug_ref, vt_ref, o_ref, qg_sc, m_sc, acc_sc, *, tq, tk):
    qi = pl.program_id(1)
    _init_state_t(m_sc, acc_sc)
    _group_queries(qg_sc, lambda h: qt_ref[h], tq)

    def tile(ki, bias):
        kt = kaug_ref[pl.ds(pl.multiple_of(ki * tk, tk), tk), :]
        for c in range(N_CHAINS):
            s = _dot(kt, qg_sc[c])
            if bias is not None:
                s = s + bias
            _update_t(c, s, vt_ref.at[ki], m_sc, acc_sc)

    _for_tiles(qi, lambda ki: tile(ki, None))
    tile(qi, _per_head(_causal_bias_t(tk, tq)))
    _write_t(o_ref, acc_sc, tq)


def _vt_spec(nk, tk):
    return pl.BlockSpec((nk, N_KV_HEADS, VT_ROWS, tk), lambda b, i: (b, 0, 0, 0))


def _fox_prompt_call(qpt, kaug, vt, batch, t_len, tq):
    n = kaug.shape[0]
    nq = t_len // tq
    return pl.pallas_call(
        functools.partial(_fox_prompt_kernel, tq=tq, tk=tq),
        grid=(batch, nq),
        in_specs=[pl.BlockSpec((N_HEADS, 2 * LANES, tq), lambda b, i: (0, 0, b * nq + i)),
                  pl.BlockSpec((t_len, 2 * LANES), lambda b, i: (b, 0)),
                  _vt_spec(nq, tq)],
        out_specs=pl.BlockSpec((BRANCH_W, tq), lambda b, i: (0, b * nq + i)),
        out_shape=jax.ShapeDtypeStruct((BRANCH_W, n), BF16),
        scratch_shapes=_attn_scratch(tq, 2 * LANES),
        compiler_params=_prompt_params(),
        name="fox_prompt",
    )(qpt, kaug, vt)


def _sortable_key(score):
    bits = lax.bitcast_convert_type(score + 0.0, jnp.int32)
    return jnp.where(bits < 0, bits ^ jnp.int32(0x7FFFFFFF), bits)


def _selected(key, idx, thr, cutoff):
    sel = jnp.logical_or(key > thr, jnp.logical_and(key == thr, idx <= cutoff))
    return jnp.logical_and(sel, key != INT_MIN)


COUNT_UNKNOWN = float(2 ** 30)


def _kth_largest(count_ge, shape, k):
    c0 = count_ge(jnp.zeros(shape, jnp.int32))
    c_pos = count_ge(jnp.ones(shape, jnp.int32))
    zero_tie = jnp.logical_and(c0 >= k, c_pos < k)
    ans = jnp.where(c0 >= k, 0, INT_MIN).astype(jnp.int32)
    cnt = jnp.where(c0 >= k, c0, COUNT_UNKNOWN)

    def is_open(cnt):
        return (jnp.max(jnp.where(zero_tie, k, cnt)) > k).astype(jnp.int32)

    def step(i, state):
        ans, cnt = state
        cand = ans + lax.shift_left(jnp.int32(1), jnp.int32(30) - i)
        c = count_ge(cand)
        take = c >= k
        return jnp.where(take, cand, ans), jnp.where(take, c, cnt)

    n_blind = 15
    ans, cnt = lax.fori_loop(0, n_blind, step, (ans, cnt))

    def cond(state):
        i, _, _, still_open = state
        return jnp.logical_and(i < 31, still_open > 0)

    def body(state):
        i, ans, cnt, _ = state
        ans, cnt = step(i + 1, step(i, (ans, cnt)))
        return i + 2, ans, cnt, is_open(cnt)

    _, ans, cnt, _ = lax.while_loop(cond, body, (jnp.int32(n_blind), ans, cnt, is_open(cnt)))
    return ans, cnt, c_pos, zero_tie


def _tie_cutoff(count_tie_lt, need, shape, nbits):
    def body(i, x):
        cand = x + lax.shift_left(jnp.int32(1), jnp.int32(nbits - 1) - i)
        return jnp.where(count_tie_lt(cand) < need, cand, x)

    return lax.fori_loop(0, nbits, body, jnp.zeros(shape, jnp.int32))


def _dsa_prompt_kernel(qt_ref, qit_ref, misct_ref, kidx_ref, k_ref, vt_ref, lstrict_ref, o_ref,
                       s_sc, before_sc, qig_sc, qg_sc, m_sc, acc_sc, *, tq, tk, n_sel):
    qi = pl.program_id(1)
    nk = qi + 1
    key_row = lax.broadcasted_iota(jnp.int32, (tk, tq), 0)
    qry_col = lax.broadcasted_iota(jnp.int32, (tk, tq), 1)
    row3 = _rows8(key_row)
    for h in range(IDX_HEADS):
        qig_sc[:, h * tq:(h + 1) * tq] = qit_ref[h * LANES:(h + 1) * LANES, :]
    wrow = jnp.concatenate([misct_ref[MISC_W + h:MISC_W + h + 1, :] for h in range(IDX_HEADS)], axis=1)

    def score_tile(ki):
        kt = kidx_ref[pl.ds(pl.multiple_of(ki * tk, tk), tk), :]
        r = jnp.maximum(_dot(kt, qig_sc[...]), 0.0) * wrow
        acc = r[:, 0:tq]
        for h in range(1, IDX_HEADS):
            acc = acc + r[:, h * tq:(h + 1) * tq]
        return _sortable_key(acc)

    def fill(ki):
        s_sc[ki] = score_tile(ki)

    _for_tiles(qi, fill)
    s_sc[qi] = jnp.where(key_row <= qry_col, score_tile(qi), INT_MIN)

    def counter(pred):
        def count(arg):
            def one(ki, acc):
                hit = pred(_rows8(s_sc[ki]), ki * tk + row3, arg[None])
                return acc + jnp.sum(jnp.where(hit, 1.0, 0.0), axis=0)

            acc = lax.fori_loop(0, nk // 2, lambda i, a: one(2 * i + 1, one(2 * i, a)),
                                jnp.zeros((SUBLANES, tq), F32))
            acc = lax.cond(nk % 2 == 1, lambda a: one(nk - 1, a), lambda a: a, acc)
            return _allreduce8(acc, jnp.add)
        return count

    kf = float(n_sel)
    stat = (SUBLANES, tq)
    thr, cnt_ge, c_pos, zero_tie = _kth_largest(counter(lambda key, idx, x: key >= x), stat, kf)

    general = jnp.logical_and(jnp.logical_and(cnt_ge > kf, jnp.logical_not(zero_tie)), thr != INT_MIN)
    need_general = lax.cond(
        jnp.max(jnp.where(general, 1.0, 0.0)) > 0.0,
        lambda: kf - counter(lambda key, idx, x: key > x)(thr),
        lambda: jnp.zeros(stat, F32))
    need = jnp.where(zero_tie, kf - c_pos, jnp.where(general, need_general, COUNT_UNKNOWN))

    def ties_before(ki, run):
        before_sc[ki] = run
        tie = _rows8(s_sc[ki]) == thr[None]
        return run + _allreduce8(jnp.sum(jnp.where(tie, 1.0, 0.0), axis=0), jnp.add)

    lax.fori_loop(0, nk, ties_before, jnp.zeros(stat, F32))

    def to_bias(ki):
        key = _rows8(s_sc[ki])
        tie = key == thr[None]
        earlier = _dot(lstrict_ref[...], jnp.where(tie, 1.0, 0.0).reshape(tk, tq).astype(BF16))
        rank = _rows8(earlier) + before_sc[ki][None]
        sel = jnp.logical_or(key > thr[None], jnp.logical_and(tie, rank < need[None]))
        sel = jnp.logical_and(sel, key != INT_MIN)
        bias = jnp.where(sel, 0.0, NEG_INF).astype(F32).reshape(tk, tq)
        s_sc[ki] = lax.bitcast_convert_type(bias, jnp.int32)

    _for_tiles(nk, to_bias)

    _init_state_t(m_sc, acc_sc)
    _group_queries(qg_sc, lambda h: qt_ref[h * LANES:(h + 1) * LANES, :], tq)

    def attend(ki):
        bias = _per_head(lax.bitcast_convert_type(s_sc[ki], F32))
        kt = k_ref[pl.ds(pl.multiple_of(ki * tk, tk), tk), :]
        for c in range(N_CHAINS):
            _update_t(c, _dot(kt, qg_sc[c]) + bias, vt_ref.at[ki], m_sc, acc_sc)

    _for_tiles(nk, attend)
    _write_t(o_ref, acc_sc, tq)


def _dsa_prompt_call(qat, qit, misct, kidx, kva, vt, batch, t_len, tq):
    n = kidx.shape[0]
    nq = t_len // tq
    n_sel = min(DSA_TOPK, t_len // 4)
    colblk = lambda rows: pl.BlockSpec((rows, tq), lambda b, i: (0, b * nq + i))
    seq = lambda width: pl.BlockSpec((t_len, width), lambda b, i: (b, 0))
    lstrict = np.tril(np.ones((tq, tq), np.float32), -1)
    return pl.pallas_call(
        functools.partial(_dsa_prompt_kernel, tq=tq, tk=tq, n_sel=n_sel),
        grid=(batch, nq),
        in_specs=[colblk(N_HEADS * LANES), colblk(IDX_HEADS * LANES), colblk(LANES),
                  seq(LANES), seq(LANES), _vt_spec(nq, tq),
                  pl.BlockSpec((tq, tq), lambda b, i: (0, 0))],
        out_specs=colblk(BRANCH_W),
        out_shape=jax.ShapeDtypeStruct((BRANCH_W, n), BF16),
        scratch_shapes=[pltpu.VMEM((nq, tq, tq), jnp.int32),
                        pltpu.VMEM((nq, SUBLANES, tq), F32),
                        pltpu.VMEM((LANES, IDX_HEADS * tq), BF16)] + _attn_scratch(tq, LANES),
        compiler_params=_prompt_params(),
        name="dsa_prompt",
    )(qat, qit, misct, kidx, kva, vt, jnp.asarray(lstrict, BF16))


def _kmean_kernel(k_ref, o_ref):
    o_ref[...] = jnp.mean(k_ref[...], axis=0, keepdims=True)


def _kmean_call(kv_f32, n_blocks):
    return pl.pallas_call(
        _kmean_kernel,
        grid=(n_blocks,),
        in_specs=[pl.BlockSpec((MOBA_BLOCK, LANES), lambda i: (i, 0))],
        out_specs=pl.BlockSpec((None, 1, LANES), lambda i: (i, 0, 0)),
        out_shape=jax.ShapeDtypeStruct((n_blocks, 1, LANES), F32),
        name="moba_kmean",
    )(kv_f32)


def _colmax(x):
    return _allreduce8(jnp.max(_rows8(x), axis=0), jnp.maximum)


def _top_blocks_bias_t(gate, n_valid_lt, n_sel):
    blk_i = lax.broadcasted_iota(jnp.int32, gate.shape, 0)
    blk = blk_i.astype(F32)
    cur3 = _rows8(jnp.where(blk_i < n_valid_lt, gate, NEG_INF))
    blk3 = _rows8(blk)
    bias3 = jnp.full(cur3.shape, NEG_INF, F32)
    for _ in range(n_sel):
        mx = _allreduce8(jnp.max(cur3, axis=0), jnp.maximum)
        first = _allreduce8(jnp.min(jnp.where(cur3 == mx[None], blk3, float(LANES)), axis=0),
                            jnp.minimum)
        pick = jnp.logical_and(mx[None] > NEG_INF, blk3 == first[None])
        bias3 = jnp.where(pick, 0.0, bias3)
        cur3 = jnp.where(pick, NEG_INF, cur3)
    return bias3.reshape(gate.shape)


def _moba_prompt_kernel(qt_ref, kmean_ref, k_ref, vt_ref, o_ref, selb_sc, qg_sc, m_sc, acc_sc,
                        *, tq, n_sel):
    qi = pl.program_id(1)
    tk = tq
    width = CHAIN * tq
    n_blocks = selb_sc.shape[1]
    km = kmean_ref[...]
    _group_queries(qg_sc, lambda h: qt_ref[h * LANES:(h + 1) * LANES, :], tq)
    for c in range(N_CHAINS):
        bias = _top_blocks_bias_t(_dot(km, qg_sc[c]), qi, n_sel)
        for b in range(n_blocks):
            selb_sc[c, b] = jnp.broadcast_to(bias[b:b + 1, :], (SUBLANES, width))
    _init_state_t(m_sc, acc_sc)

    def tile(ki, causal):
        kt = k_ref[pl.ds(pl.multiple_of(ki * tk, tk), tk), :]
        for c in range(N_CHAINS):
            s = _dot(kt, qg_sc[c])
            if causal is not None:
                s = s + causal
            else:
                s = (_rows8(s) + selb_sc[c, ki][None]).reshape(tk, width)
            _update_t(c, s, vt_ref.at[ki], m_sc, acc_sc)

    tile(qi, _per_head(_causal_bias_t(tk, tq)))
    _for_tiles(qi, lambda ki: tile(ki, None))
    _write_t(o_ref, acc_sc, tq)


def _moba_prompt_call(qbt, kmean, kvb, vt, batch, t_len):
    n = kvb.shape[0]
    tq = MOBA_BLOCK
    nq = t_len // tq
    n_sel = min(MOBA_TOPK, nq)
    colblk = lambda rows: pl.BlockSpec((rows, tq), lambda b, i: (0, b * nq + i))
    return pl.pallas_call(
        functools.partial(_moba_prompt_kernel, tq=tq, n_sel=n_sel),
        grid=(batch, nq),
        in_specs=[colblk(N_HEADS * LANES),
                  pl.BlockSpec((None, LANES, LANES), lambda b, i: (b, 0, 0)),
                  pl.BlockSpec((t_len, LANES), lambda b, i: (b, 0)),
                  _vt_spec(nq, tq)],
        out_specs=colblk(BRANCH_W),
        out_shape=jax.ShapeDtypeStruct((BRANCH_W, n), BF16),
        scratch_shapes=[pltpu.VMEM((N_CHAINS, nq, SUBLANES, CHAIN * tq), F32)] + _attn_scratch(tq, LANES),
        compiler_params=_prompt_params(),
        name="moba_prompt",
    )(qbt, kmean, kvb, vt)


def _merge_kernel(x_ref, oa_ref, ob_ref, oc_ref, sz_ref, gate_ref, wbr_ref, wout_ref, gfin_ref,
                  o_ref, *, final):
    m = None
    for n, o_n in enumerate((oa_ref, ob_ref, oc_ref)):
        a = o_n[...] * sz_ref[:, n * BRANCH_W:(n + 1) * BRANCH_W]
        u = _dot(a, wbr_ref[n])
        t = gate_ref[:, n * D_MODEL:(n + 1) * D_MODEL].astype(F32) * u
        m = t if m is None else m + t
    y = x_ref[...] + _dot(m.astype(BF16), wout_ref[...])
    if final:
        ms = jnp.mean(y * y, axis=-1, keepdims=True)
        y = y * lax.rsqrt(ms + RMS_EPS) * gfin_ref[...]
    o_ref[...] = y


def _merge_call(x, oa, ob, oc, sz, gates, wbr, wout, gfin, tm, final):
    n = x.shape[0]
    row = lambda width: pl.BlockSpec((tm, width), lambda i: (i, 0))
    return pl.pallas_call(
        functools.partial(_merge_kernel, final=final),
        grid=(n // tm,),
        in_specs=[row(D_MODEL), row(BRANCH_W), row(BRANCH_W), row(BRANCH_W),
                  row(N_BRANCH * BRANCH_W), row(N_BRANCH * D_MODEL),
                  pl.BlockSpec((N_BRANCH, BRANCH_W, D_MODEL), lambda i: (0, 0, 0)),
                  pl.BlockSpec((D_MODEL, D_MODEL), lambda i: (0, 0)),
                  pl.BlockSpec((1, D_MODEL), lambda i: (0, 0))],
        out_specs=row(D_MODEL),
        out_shape=jax.ShapeDtypeStruct((n, D_MODEL), F32),
        compiler_params=pltpu.CompilerParams(dimension_semantics=("arbitrary",),
                                             vmem_limit_bytes=VMEM_LIMIT),
        name="merge",
    )(x, oa, ob, oc, sz, gates, wbr, wout, gfin)


def _expand8(x):
    return jnp.concatenate([jnp.broadcast_to(x[h:h + 1, :], (SUBLANES, x.shape[1]))
                            for h in range(x.shape[0])], axis=0)


def _tile8(x, reps):
    return jnp.concatenate([x] * reps, axis=0)


def _new_key_bias(rows, n_new):
    t8 = lax.broadcasted_iota(jnp.int32, (rows, LANES), 0) % SUBLANES
    lane = lax.broadcasted_iota(jnp.int32, (rows, LANES), 1)
    return jnp.where(lane <= jnp.minimum(t8, n_new - 1), 0.0, NEG_INF).astype(F32)


def _init_state_d(m_sc, l_sc, acc_sc):
    m_sc[...] = jnp.full(m_sc.shape, NEG_INF, F32)
    l_sc[...] = jnp.zeros(l_sc.shape, F32)
    acc_sc[...] = jnp.zeros(acc_sc.shape, F32)


def _update_d(s, pv, m_sc, l_sc, acc_sc):
    m_prev = m_sc[...]
    m_new = jnp.maximum(m_prev, jnp.max(s, axis=1, keepdims=True))
    m_safe = jnp.where(m_new == NEG_INF, 0.0, m_new)
    p = jnp.exp2(s - m_safe)
    alpha = jnp.exp2(m_prev - m_safe)
    l_sc[...] = alpha * l_sc[...] + jnp.sum(p, axis=1, keepdims=True)
    acc_sc[...] = alpha * acc_sc[...] + pv(p.astype(BF16))
    m_sc[...] = m_new


def _page_logits(q, pages):
    return _dot(q, jnp.concatenate([pg[0].astype(BF16) for pg in pages], axis=1))


def _page_pv(pages):
    def pv(p):
        acc = None
        for j, pg in enumerate(pages):
            t = _dot_nt(p[:, j * LANES:(j + 1) * LANES], pg[1].astype(BF16))
            acc = t if acc is None else acc + t
        return acc
    return pv


def _decode_state():
    rows = N_HEADS * SUBLANES
    return [pltpu.VMEM((rows, 1), F32), pltpu.VMEM((rows, 1), F32), pltpu.VMEM((rows, LANES), F32)]


def _dsa_decode_kernel(pt_ref, qi_ref, w_ref, q_ref, knew_idx_ref, knew_ref, vnew_ref, *rest,
                       n_pg, n_groups, n_sel, n_new, nbits):
    idx_pages = rest[:n_pg]
    kv_pages = rest[n_pg:2 * n_pg]
    o_ref, s_sc, thr_sc, cut_sc, m_sc, l_sc, acc_sc = rest[2 * n_pg:]
    ph = pl.program_id(1)
    g = pl.program_id(2)
    width = n_pg * LANES
    lane_w = lax.broadcasted_iota(jnp.int32, (SUBLANES, width), 1)

    def scores(r):
        r = jnp.maximum(r, 0.0) * w_ref[:, 0:1]
        acc = r[0:SUBLANES]
        for h in range(1, IDX_HEADS):
            acc = acc + r[h * SUBLANES:(h + 1) * SUBLANES]
        return _sortable_key(acc)

    @pl.when(ph == 0)
    def _():
        qi = qi_ref[:, 0:IDX_DIM]
        r = _dot(qi, jnp.concatenate([pg[...].astype(BF16) for pg in idx_pages], axis=1))
        s_sc[g] = scores(r)

    @pl.when(jnp.logical_and(ph == 0, g == n_groups - 1))
    def _():
        key = scores(_dot_nt(qi_ref[...], knew_idx_ref[...]))
        visible = _new_key_bias(SUBLANES, n_new) == 0.0
        t8 = lax.broadcasted_iota(jnp.int32, (SUBLANES, LANES), 0)
        key = jnp.where(jnp.logical_and(visible, t8 < n_new), key, INT_MIN)
        s_sc[n_groups] = jnp.full((SUBLANES, width), INT_MIN, jnp.int32)
        s_sc[n_groups, :, 0:LANES] = key

        def counter(pred):
            def count(arg):
                acc = jnp.zeros((SUBLANES, width), F32)
                for gi in range(n_groups + 1):
                    acc = acc + jnp.where(pred(s_sc[gi], gi * width + lane_w, arg), 1.0, 0.0)
                return jnp.sum(acc, axis=1, keepdims=True)
            return count

        kf = float(n_sel)
        stat = (SUBLANES, 1)
        thr, _, _, _ = _kth_largest(counter(lambda key, idx, x: key >= x), stat, kf)
        need = kf - counter(lambda key, idx, x: key > x)(thr)
        tie_lt = counter(lambda key, idx, x: jnp.logical_and(key == thr, idx < x))
        cut = _tie_cutoff(tie_lt, need, stat, nbits)
        thr_sc[...] = jnp.broadcast_to(thr, (SUBLANES, LANES))
        cut_sc[...] = jnp.broadcast_to(cut, (SUBLANES, LANES))
        _init_state_d(m_sc, l_sc, acc_sc)

    def sel_bias(key, idx):
        sel = _selected(key, idx, thr_sc[:, 0:1], cut_sc[:, 0:1])
        return _tile8(jnp.where(sel, 0.0, NEG_INF).astype(F32), N_HEADS)

    @pl.when(ph == 1)
    def _():
        s = _page_logits(q_ref[...], kv_pages) + sel_bias(s_sc[g], g * width + lane_w)
        _update_d(s, _page_pv(kv_pages), m_sc, l_sc, acc_sc)

    @pl.when(jnp.logical_and(ph == 1, g == n_groups - 1))
    def _():
        key = s_sc[n_groups, :, 0:LANES]
        s = _dot_nt(q_ref[...], knew_ref[...]) + sel_bias(key, n_groups * width + lane_w[:, 0:LANES])
        _update_d(s, lambda p: _dot(p, vnew_ref[...]), m_sc, l_sc, acc_sc)
        o_ref[...] = acc_sc[...] / l_sc[...]


def _page_specs(n_pg, block, index_of):
    return [pl.BlockSpec(block, functools.partial(index_of, j)) for j in range(n_pg)]


KV_PAGE = (None, 2, 2 * HEAD_DIM, LANES)


def _dsa_decode_call(page_table, layer_off, qi_rows, w_rows, q_rows, knew_idx, knew, vnew,
                     cache_idx, cache_kv, n_pg):
    nb, n_pages = page_table.shape
    n_groups = n_pages // n_pg
    n_new = 4
    n_sel = min(DSA_TOPK, (n_pages * LANES + n_new) // 4)
    nbits = max(1, ((n_groups + 1) * n_pg * LANES - 1).bit_length())

    def idx_map(j, b, ph, g, pt):
        gg = jnp.where(ph == 0, g, n_groups - 1)
        return (pt[b, gg * n_pg + j] + layer_off, 0, 0)

    def kv_map(j, b, ph, g, pt):
        gg = jnp.where(ph == 1, g, 0)
        return (pt[b, gg * n_pg + j] + layer_off, 0, 0, 0)

    per_b = lambda r, c: pl.BlockSpec((None, r, c), lambda b, ph, g, pt: (b, 0, 0))
    rows_q = N_HEADS * SUBLANES
    rows_i = IDX_HEADS * SUBLANES
    grid_spec = pltpu.PrefetchScalarGridSpec(
        num_scalar_prefetch=1,
        grid=(nb, 2, n_groups),
        in_specs=[per_b(rows_i, LANES), per_b(rows_i, LANES),
                  per_b(rows_q, LANES), per_b(LANES, LANES), per_b(LANES, LANES), per_b(LANES, LANES)]
        + _page_specs(n_pg, (None, IDX_DIM, LANES), idx_map)
        + _page_specs(n_pg, KV_PAGE, kv_map),
        out_specs=per_b(rows_q, LANES),
        scratch_shapes=[pltpu.VMEM((n_groups + 1, SUBLANES, n_pg * LANES), jnp.int32),
                        pltpu.VMEM((SUBLANES, LANES), jnp.int32), pltpu.VMEM((SUBLANES, LANES), jnp.int32)]
        + _decode_state(),
    )
    return pl.pallas_call(
        functools.partial(_dsa_decode_kernel, n_pg=n_pg, n_groups=n_groups, n_sel=n_sel,
                          n_new=n_new, nbits=nbits),
        grid_spec=grid_spec,
        out_shape=jax.ShapeDtypeStruct((nb, rows_q, LANES), F32),
        compiler_params=pltpu.CompilerParams(
            dimension_semantics=("arbitrary", "arbitrary", "arbitrary"), vmem_limit_bytes=VMEM_LIMIT),
        name="dsa_decode",
    )(page_table, qi_rows, w_rows, q_rows, knew_idx, knew, vnew,
      *([cache_idx] * n_pg), *([cache_kv] * n_pg))


def _moba_decode_kernel(pt_ref, q_ref, knew_ref, vnew_ref, *rest, n_pg, n_groups, n_sel, n_new):
    kv_pages = rest[:n_pg]
    o_ref, gate_sc, mb_sc, lb_sc, accb_sc, m_sc, l_sc, acc_sc = rest[n_pg:]
    g = pl.program_id(1)
    ppb = MOBA_BLOCK // LANES
    blocks_per_step = n_pg // ppb
    n_blocks = n_groups * blocks_per_step
    rows = N_HEADS * SUBLANES
    q = q_ref[...]

    for jb in range(blocks_per_step):
        pages = kv_pages[jb * ppb:(jb + 1) * ppb]
        ksum = None
        for pg in pages:
            t = jnp.sum(pg[0], axis=1, keepdims=True)
            ksum = t if ksum is None else ksum + t
        kmean = jnp.broadcast_to(ksum * (1.0 / MOBA_BLOCK), (LANES, LANES)).astype(BF16)
        gate = _dot(q, kmean)
        s = _page_logits(q, pages)
        mj = jnp.max(s, axis=1, keepdims=True)
        p = jnp.exp2(s - mj)
        b = g * blocks_per_step + jb
        gate_sc[b] = gate
        mb_sc[b] = jnp.broadcast_to(mj, (rows, LANES))
        lb_sc[b] = jnp.broadcast_to(jnp.sum(p, axis=1, keepdims=True), (rows, LANES))
        accb_sc[b] = _page_pv(pages)(p.astype(BF16))

    @pl.when(g == n_groups - 1)
    def _():
        _init_state_d(m_sc, l_sc, acc_sc)
        s = _dot_nt(q, knew_ref[...]) + _new_key_bias(rows, n_new)
        _update_d(s, lambda p: _dot(p, vnew_ref[...]), m_sc, l_sc, acc_sc)
        m_own = jnp.broadcast_to(m_sc[...], (rows, LANES))

        def pick_round(_, c):
            def mx_body(b, mx):
                return jnp.maximum(mx, gate_sc[b])
            mx = lax.fori_loop(0, n_blocks, mx_body, jnp.full((rows, LANES), NEG_INF, F32))

            def first_body(b, first):
                return jnp.minimum(first, jnp.where(gate_sc[b] == mx, b, n_blocks))
            first = lax.fori_loop(0, n_blocks, first_body, jnp.full((rows, LANES), n_blocks, jnp.int32))

            def mark_body(b, cc):
                hit = jnp.logical_and(first == b, mx > NEG_INF)
                gate_sc[b] = jnp.where(hit, NEG_INF, gate_sc[b])
                lb_sc[b] = jnp.where(hit, -lb_sc[b], lb_sc[b])
                return cc
            lax.fori_loop(0, n_blocks, mark_body, 0)
            return c

        lax.fori_loop(0, n_sel, pick_round, 0)

        def max_body(b, mt):
            return jnp.maximum(mt, jnp.where(lb_sc[b] < 0.0, mb_sc[b], NEG_INF))
        m_tot = lax.fori_loop(0, n_blocks, max_body, m_own)

        def comb_body(b, carry):
            l_tot, acc_tot = carry
            picked = lb_sc[b] < 0.0
            wgt = jnp.where(picked, jnp.exp2(mb_sc[b] - m_tot), 0.0)
            return l_tot - wgt * lb_sc[b], acc_tot + wgt * accb_sc[b]

        w_own = jnp.exp2(m_own - m_tot)
        l0 = w_own * jnp.broadcast_to(l_sc[...], (rows, LANES))
        a0 = w_own * acc_sc[...]
        l_tot, acc_tot = lax.fori_loop(0, n_blocks, comb_body, (l0, a0))
        o_ref[...] = acc_tot / l_tot


def _moba_decode_call(page_table, layer_off, q_rows, knew, vnew, cache_kv, n_pg):
    nb, n_pages = page_table.shape
    n_groups = n_pages // n_pg
    n_blocks = n_pages * LANES // MOBA_BLOCK
    n_sel = min(MOBA_TOPK, n_blocks + 1)
    rows_q = N_HEADS * SUBLANES

    def kv_map(j, b, g, pt):
        return (pt[b, g * n_pg + j] + layer_off, 0, 0, 0)

    per_b = lambda r, c: pl.BlockSpec((None, r, c), lambda b, g, pt: (b, 0, 0))
    slab = lambda: pltpu.VMEM((n_blocks, rows_q, LANES), F32)
    grid_spec = pltpu.PrefetchScalarGridSpec(
        num_scalar_prefetch=1,
        grid=(nb, n_groups),
        in_specs=[per_b(rows_q, LANES), per_b(LANES, LANES), per_b(LANES, LANES)]
        + _page_specs(n_pg, KV_PAGE, kv_map),
        out_specs=per_b(rows_q, LANES),
        scratch_shapes=[slab(), slab(), slab(), slab()] + _decode_state(),
    )
    return pl.pallas_call(
        functools.partial(_moba_decode_kernel, n_pg=n_pg, n_groups=n_groups, n_sel=n_sel, n_new=4),
        grid_spec=grid_spec,
        out_shape=jax.ShapeDtypeStruct((nb, rows_q, LANES), F32),
        compiler_params=pltpu.CompilerParams(dimension_semantics=("arbitrary", "arbitrary"),
                                             vmem_limit_bytes=VMEM_LIMIT),
        name="moba_decode",
    )(page_table, q_rows, knew, vnew, *([cache_kv] * n_pg))


def _fox_decode_kernel(pt_ref, q_ref, knew_ref, vnew_ref, lfnew_rows_ref, lfnew_t_ref,
                       tri_ref, upper_ref, sfx_ref, *rest, n_pg, n_groups, n_new):
    kv_pages = rest[:n_pg]
    lf_pages = rest[n_pg:2 * n_pg]
    o_ref, cq_sc, carry_sc, m_sc, l_sc, acc_sc = rest[2 * n_pg:]
    g = pl.program_id(1)
    rows = N_HEADS * SUBLANES
    q = q_ref[...]

    @pl.when(g == 0)
    def _():
        cq_rows = _dot3_left(tri_ref[...], lfnew_rows_ref[...]) * LOG2E
        cq_sc[...] = cq_rows
        new_cum_t = _dot3(lfnew_t_ref[...], upper_ref[...]) * LOG2E
        carry_sc[...] = jnp.zeros(carry_sc.shape, F32)
        _init_state_d(m_sc, l_sc, acc_sc)
        s = (_dot_nt(q, knew_ref[...]) + cq_rows - _expand8(new_cum_t)
             + _new_key_bias(rows, n_new))
        _update_d(s, lambda p: _dot(p, vnew_ref[...]), m_sc, l_sc, acc_sc)

    carry = carry_sc[...]
    suffix = [None] * n_pg
    for j in range(n_pg - 1, -1, -1):
        lft = lf_pages[j][...]
        suffix[j] = (_dot3(lft, sfx_ref[...]) + carry) * LOG2E
        carry = carry + jnp.sum(lft, axis=1, keepdims=True)
    carry_sc[...] = carry
    bias = jnp.concatenate([_expand8(sf) for sf in suffix], axis=1)
    s = _page_logits(q, kv_pages) + bias + jnp.concatenate([cq_sc[...]] * n_pg, axis=1)
    _update_d(s, _page_pv(kv_pages), m_sc, l_sc, acc_sc)

    @pl.when(g == n_groups - 1)
    def _():
        o_ref[...] = acc_sc[...] / l_sc[...]


def _fox_decode_call(page_table, layer_off, q_rows, knew, vnew, lfnew_rows, lfnew_t,
                     cache_kv, cache_lft, n_pg):
    nb, n_pages = page_table.shape
    n_groups = n_pages // n_pg
    rows_q = N_HEADS * SUBLANES
    r = np.arange(rows_q)
    tri = ((r[:, None] // SUBLANES == r[None, :] // SUBLANES)
           & (r[None, :] % SUBLANES <= r[:, None] % SUBLANES)).astype(np.float32)
    lane = np.arange(LANES)
    upper = (lane[:, None] <= lane[None, :]).astype(np.float32)
    sfx = (lane[:, None] > lane[None, :]).astype(np.float32)

    def kv_map(j, b, g, pt):
        return (pt[b, (n_groups - 1 - g) * n_pg + j] + layer_off, 0, 0, 0)

    def lf_map(j, b, g, pt):
        return (pt[b, (n_groups - 1 - g) * n_pg + j] + layer_off, 0, 0)

    per_b = lambda rr, c: pl.BlockSpec((None, rr, c), lambda b, g, pt: (b, 0, 0))
    const = lambda rr, c: pl.BlockSpec((rr, c), lambda b, g, pt: (0, 0))
    grid_spec = pltpu.PrefetchScalarGridSpec(
        num_scalar_prefetch=1,
        grid=(nb, n_groups),
        in_specs=[per_b(rows_q, LANES), per_b(LANES, LANES), per_b(LANES, LANES),
                  per_b(rows_q, LANES), per_b(SUBLANES, LANES),
                  const(rows_q, rows_q), const(LANES, LANES), const(LANES, LANES)]
        + _page_specs(n_pg, KV_PAGE, kv_map)
        + _page_specs(n_pg, (None, SUBLANES, LANES), lf_map),
        out_specs=per_b(rows_q, LANES),
        scratch_shapes=[pltpu.VMEM((rows_q, LANES), F32), pltpu.VMEM((SUBLANES, LANES), F32)]
        + _decode_state(),
    )
    return pl.pallas_call(
        functools.partial(_fox_decode_kernel, n_pg=n_pg, n_groups=n_groups, n_new=4),
        grid_spec=grid_spec,
        out_shape=jax.ShapeDtypeStruct((nb, rows_q, LANES), F32),
        compiler_params=pltpu.CompilerParams(dimension_semantics=("arbitrary", "arbitrary"),
                                             vmem_limit_bytes=VMEM_LIMIT),
        name="fox_decode",
    )(page_table, q_rows, knew, vnew, lfnew_rows, lfnew_t,
      jnp.asarray(tri, BF16), jnp.asarray(upper, BF16), jnp.asarray(sfx, BF16),
      *([cache_kv] * n_pg), *([cache_lft] * n_pg))


def _rope_tables(pos):
    half = HEAD_DIM // 2
    freqs = ROPE_THETA ** (-jnp.arange(half, dtype=F32) / half)
    ang = pos.astype(F32)[:, None] * freqs[None, :]
    cos = jnp.tile(jnp.cos(ang), (1, LANES // half))
    sin = jnp.sin(ang)
    sin = jnp.tile(jnp.concatenate([-sin, sin], axis=1), (1, LANES // HEAD_DIM))
    return cos, sin


def _layout_w_in(w_in_l):
    cols = jnp.asarray(np.maximum(W_SRC, 0), jnp.int32)
    keep = jnp.asarray(W_SRC >= 0)
    return jnp.where(keep[None, :], jnp.take(w_in_l, cols, axis=1), 0.0).astype(BF16)


def _decode_rows(a, nb, t_new, heads):
    a = a.reshape(nb, t_new, heads, LANES).transpose(0, 2, 1, 3)
    a = jnp.pad(a, ((0, 0), (0, 0), (0, SUBLANES - t_new), (0, 0)))
    return a.reshape(nb, heads * SUBLANES, LANES)


def _new_rows(a, nb, t_new):
    a = a.reshape(nb, t_new, LANES)
    return jnp.pad(a, ((0, 0), (0, LANES - t_new), (0, 0)))


def _pick_pages_per_step(n_pages):
    for c in (32, 16, 8, 4, 2):
        if n_pages % c == 0:
            return c
    raise ValueError("page count must be even")


def _value_tiles(kv, tk):
    n = kv.shape[0]
    v = kv[:, KV_W:2 * KV_W].reshape(n // tk, tk, N_KV_HEADS, HEAD_DIM).transpose(0, 2, 3, 1)
    tail = jnp.zeros((n // tk, N_KV_HEADS, VT_ROWS - HEAD_DIM, tk), BF16).at[:, :, 0, :].set(1.0)
    return jnp.concatenate([v, tail], axis=2)


def _prompt_branches(proj, batch, t_len):
    (qa, qb, qi, qc, kva, kvb, kvc, kidx, _, _, bkv, _, _, misc, _, _) = proj
    tq = MOBA_BLOCK
    n = qa.shape[0]
    n_blocks = t_len // MOBA_BLOCK
    o_a = _dsa_prompt_call(qa.T, qi.T, misc.T, kidx, kva, _value_tiles(kva, tq), batch, t_len, tq)
    kmean = _kmean_call(bkv, batch * n_blocks).reshape(batch, n_blocks, LANES)
    kmean = jnp.pad(kmean, ((0, 0), (0, LANES - n_blocks), (0, 0))).astype(BF16)
    o_b = _moba_prompt_call(qb.T, kmean, kvb, _value_tiles(kvb, tq), batch, t_len)
    kaug, qaug = _foxprep_call(misc, kvc, batch, t_len, LANES)
    qpt = jnp.concatenate([qc.T.reshape(N_HEADS, LANES, n), qaug.T.reshape(N_HEADS, LANES, n)], axis=1)
    o_c = _fox_prompt_call(qpt, kaug, _value_tiles(kvc, tq), batch, t_len, tq)
    return o_a.T, o_b.T, o_c.T


def _sample_branches(proj, nb, t_new, page_table, layer_off, caches):
    (qa, qb, qi, qc, kva, kvb, kvc, kidx, _, _, _, _, logf, misc, _, _) = proj
    ca_idx, ca_kv, cb_kv, cc_kv, cc_lft = caches
    n_pg = _pick_pages_per_step(page_table.shape[1])

    def head_rows(col, heads):
        a = col.reshape(nb, t_new, heads).transpose(0, 2, 1)
        a = jnp.pad(a, ((0, 0), (0, 0), (0, SUBLANES - t_new)))
        return jnp.broadcast_to(a.reshape(nb, heads * SUBLANES, 1), (nb, heads * SUBLANES, LANES))

    def new_kv(kv):
        return _new_rows(kv[:, 0:LANES], nb, t_new), _new_rows(kv[:, LANES:2 * LANES], nb, t_new)

    d_a = _dsa_decode_call(
        page_table, layer_off, _decode_rows(qi, nb, t_new, IDX_HEADS),
        head_rows(misc[:, MISC_W:MISC_W + IDX_HEADS], IDX_HEADS),
        _decode_rows(qa, nb, t_new, N_HEADS), _new_rows(kidx, nb, t_new), *new_kv(kva),
        ca_idx, ca_kv, n_pg)
    d_b = _moba_decode_call(page_table, layer_off, _decode_rows(qb, nb, t_new, N_HEADS),
                            *new_kv(kvb), cb_kv, n_pg)
    lf_t = jnp.pad(logf.reshape(nb, t_new, N_HEADS).transpose(0, 2, 1),
                   ((0, 0), (0, 0), (0, LANES - t_new)))
    d_c = _fox_decode_call(page_table, layer_off, _decode_rows(qc, nb, t_new, N_HEADS),
                           *new_kv(kvc), head_rows(logf, N_HEADS), lf_t, cc_kv, cc_lft, n_pg)

    def compact(d):
        d = d.reshape(nb, N_KV_HEADS, GROUP, SUBLANES, N_KV_HEADS, HEAD_DIM)[:, :, :, :t_new]
        d = jnp.stack([d[:, gidx, :, :, gidx, :] for gidx in range(N_KV_HEADS)], axis=1)
        return d.transpose(0, 3, 1, 2, 4).reshape(nb * t_new, BRANCH_W).astype(BF16)

    return compact(d_a), compact(d_b), compact(d_c)


def _native_pages(cache, depth, n_pool):
    nd = cache.ndim
    c = jnp.transpose(cache, (0, 1) + tuple(range(3, nd)) + (2,))
    return c.reshape((depth * n_pool,) + c.shape[2:])


def kernel(x_prompt, x_sample, cache_a_kv, cache_a_idx, cache_b_kv, cache_c_kv, cache_c_logf,
           page_table, norm_g, w_in, b_forget, w_branch, w_out, final_norm_g):
    batch, t_len, _ = x_prompt.shape
    nb, t_new, _ = x_sample.shape
    depth = norm_g.shape[0]
    n_pool, page_size = cache_a_kv.shape[1], cache_a_kv.shape[2]
    n_pages = page_table.shape[1]
    past_len = n_pages * page_size
    assert page_size == LANES and t_len % MOBA_BLOCK == 0 and past_len % MOBA_BLOCK == 0 and t_new == 4
    n_p = batch * t_len
    n_s = nb * t_new

    pos_p = jnp.tile(jnp.arange(t_len, dtype=jnp.int32), batch)
    pos_s = jnp.tile(past_len + jnp.arange(t_new, dtype=jnp.int32), nb)
    cos_p, sin_p = _rope_tables(pos_p)
    cos_s, sin_s = _rope_tables(pos_s)

    kv_pages = lambda c: _native_pages(c, depth, n_pool).reshape(
        depth * n_pool, 2, N_KV_HEADS * HEAD_DIM, page_size)
    ca_kv, cb_kv, cc_kv = kv_pages(cache_a_kv), kv_pages(cache_b_kv), kv_pages(cache_c_kv)
    ca_idx = _native_pages(cache_a_idx, depth, n_pool)
    cc_lft = _native_pages(cache_c_logf, depth, n_pool)

    xp = x_prompt.reshape(n_p, D_MODEL)
    xs = x_sample.reshape(n_s, D_MODEL)
    gfin = final_norm_g.reshape(1, D_MODEL)
    rows_p = []
    rows_s = []
    for l in range(depth):
        w = _layout_w_in(w_in[l])
        g = norm_g[l].reshape(1, D_MODEL)
        bf_pad = jnp.zeros((1, LANES), F32).at[0, MISC_F:MISC_F + N_HEADS].set(b_forget[l])
        wbr = w_branch[l].astype(BF16)
        wout = w_out[l].astype(BF16)
        final = l == depth - 1

        proj_p = _proj_call(xp, g, w, cos_p, sin_p, bf_pad, 256)
        rows_p.append(proj_p[8:13])
        o_a, o_b, o_c = _prompt_branches(proj_p, batch, t_len)
        xp = _merge_call(xp, o_a, o_b, o_c, proj_p[14], proj_p[15], wbr, wout, gfin, 256, final)

        proj_s = _proj_call(xs, g, w, cos_s, sin_s, bf_pad, n_s)
        rows_s.append(proj_s[8:13])
        o_a, o_b, o_c = _sample_branches(proj_s, nb, t_new, page_table, l * n_pool,
                                         (ca_idx, ca_kv, cb_kv, cc_kv, cc_lft))
        xs = _merge_call(xs, o_a, o_b, o_c, proj_s[14], proj_s[15], wbr, wout, gfin, n_s, final)

    def stack(rows, i, b, t, tail):
        return jnp.stack([r[i] for r in rows]).reshape((depth, b, t) + tail)

    kv_tail = (2, N_KV_HEADS, HEAD_DIM)
    outs_p = (stack(rows_p, 0, batch, t_len, kv_tail), stack(rows_p, 1, batch, t_len, (IDX_DIM,)),
              stack(rows_p, 2, batch, t_len, kv_tail), stack(rows_p, 3, batch, t_len, kv_tail),
              stack(rows_p, 4, batch, t_len, (N_HEADS,)))
    outs_s = (stack(rows_s, 0, nb, t_new, kv_tail), stack(rows_s, 1, nb, t_new, (IDX_DIM,)),
              stack(rows_s, 2, nb, t_new, kv_tail), stack(rows_s, 3, nb, t_new, kv_tail),
              stack(rows_s, 4, nb, t_new, (N_HEADS,)))
    return (xp.reshape(batch, t_len, D_MODEL), xs.reshape(nb, t_new, D_MODEL)) + outs_p + outs_s
```

```python
import functools

import numpy as np
import jax
import jax.numpy as jnp
from jax import lax
from jax.experimental import pallas as pl
from jax.experimental.pallas import tpu as pltpu

D_MODEL = 1024
HEAD_DIM = 64
N_HEADS = 8
N_KV_HEADS = 2
GROUP = N_HEADS // N_KV_HEADS
BRANCH_W = N_HEADS * HEAD_DIM
KV_W = N_KV_HEADS * HEAD_DIM
N_BRANCH = 3
IDX_HEADS = 4
IDX_DIM = 64
DSA_TOPK = 256
MOBA_BLOCK = 256
MOBA_TOPK = 3
ROPE_THETA = 10000.0
RMS_EPS = 1e-6
ATTN_SCALE = HEAD_DIM ** -0.5
IDX_SCALE = (IDX_HEADS ** -0.5) * (IDX_DIM ** -0.5)
LOG2E = float(np.log2(np.e))
QK_SCALE = ATTN_SCALE * LOG2E

LANES = 128
SUBLANES = 8
VMEM_LIMIT = 56 * 1024 * 1024
INT_MIN = -2 ** 31
NEG_INF = float("-inf")
F32 = jnp.float32
BF16 = jnp.bfloat16

VT_ROWS = 80
VT_ONE = HEAD_DIM

IN_SPLITS = (
    ('a_q', BRANCH_W), ('a_k', KV_W), ('a_v', KV_W), ('a_z', BRANCH_W),
    ('a_iq', IDX_HEADS * IDX_DIM), ('a_ik', IDX_DIM), ('a_iw', IDX_HEADS),
    ('b_q', BRANCH_W), ('b_k', KV_W), ('b_v', KV_W), ('b_z', BRANCH_W),
    ('c_q', BRANCH_W), ('c_k', KV_W), ('c_v', KV_W), ('c_z', BRANCH_W), ('c_f', N_HEADS),
    ('gates', N_BRANCH * D_MODEL),
)
IN_OFF = {}
_o = 0
for _n, _w in IN_SPLITS:
    IN_OFF[_n] = _o
    _o += _w
IN_WIDTH = _o

MISC_F = 0
MISC_W = 8
AUG_ONE = 0
AUG_C = 3


def _wide_q_cols(name):
    src = np.full((N_HEADS * LANES,), -1, np.int64)
    for h in range(N_HEADS):
        g = h // GROUP
        dst = h * LANES + g * HEAD_DIM
        src[dst:dst + HEAD_DIM] = IN_OFF[name] + h * HEAD_DIM + np.arange(HEAD_DIM)
    return src


def _seg(name, width, pad_to=None):
    src = IN_OFF[name] + np.arange(width)
    if pad_to is not None and pad_to > width:
        src = np.concatenate([src, np.full((pad_to - width,), -1, np.int64)])
    return src


def _build_layout():
    segs = []
    segs.append(('a_qw', _wide_q_cols('a_q')))
    segs.append(('b_qw', _wide_q_cols('b_q')))
    iq = np.full((IDX_HEADS * LANES,), -1, np.int64)
    for h in range(IDX_HEADS):
        iq[h * LANES:h * LANES + IDX_DIM] = IN_OFF['a_iq'] + h * IDX_DIM + np.arange(IDX_DIM)
    segs.append(('a_iqw', iq))
    segs.append(('a_k', _seg('a_k', KV_W)))
    segs.append(('b_k', _seg('b_k', KV_W)))
    segs.append(('a_ik', _seg('a_ik', IDX_DIM, LANES)))
    segs.append(('a_v', _seg('a_v', KV_W)))
    segs.append(('b_v', _seg('b_v', KV_W)))
    segs.append(('c_qw', _wide_q_cols('c_q')))
    segs.append(('c_k', _seg('c_k', KV_W)))
    segs.append(('c_v', _seg('c_v', KV_W)))
    segs.append(('a_z', _seg('a_z', BRANCH_W)))
    segs.append(('b_z', _seg('b_z', BRANCH_W)))
    segs.append(('c_z', _seg('c_z', BRANCH_W)))
    misc = np.full((LANES,), -1, np.int64)
    misc[MISC_F:MISC_F + N_HEADS] = IN_OFF['c_f'] + np.arange(N_HEADS)
    misc[MISC_W:MISC_W + IDX_HEADS] = IN_OFF['a_iw'] + np.arange(IDX_HEADS)
    segs.append(('misc', misc))
    segs.append(('gates', _seg('gates', N_BRANCH * D_MODEL)))
    off = {}
    o = 0
    for n, s in segs:
        off[n] = (o, len(s))
        o += len(s)
    return np.concatenate([s for _, s in segs]), off, o


W_SRC, W_OFF, W_TOTAL = _build_layout()


def _dot(a, b):
    return jnp.dot(a, b, preferred_element_type=F32)


def _dot_nt(a, b):
    return lax.dot_general(a, b, (((1,), (1,)), ((), ())), preferred_element_type=F32)


def _split3(x):
    hi = x.astype(BF16)
    r1 = x - hi.astype(F32)
    mid = r1.astype(BF16)
    lo = (r1 - mid.astype(F32)).astype(BF16)
    return hi, mid, lo


def _dot3(x, m_bf16):
    hi, mid, lo = _split3(x)
    return _dot(hi, m_bf16) + _dot(mid, m_bf16) + _dot(lo, m_bf16)


def _dot3_left(m_bf16, x):
    hi, mid, lo = _split3(x)
    return _dot(m_bf16, hi) + _dot(m_bf16, mid) + _dot(m_bf16, lo)


def _proj_kernel(x_ref, g_ref, w_ref, cos_ref, sin_ref, bf_ref,
                 qa_ref, qb_ref, qi_ref, qc_ref, kva_ref, kvb_ref, kvc_ref, kidx_ref,
                 akv_ref, aidx_ref, bkv_ref, ckv_ref, logf_ref, misc_ref, sz_ref, gate_ref):
    x = x_ref[...]
    ms = jnp.mean(x * x, axis=-1, keepdims=True)
    h = (x * lax.rsqrt(ms + RMS_EPS) * g_ref[...]).astype(BF16)
    cos = cos_ref[...]
    sin = sin_ref[...]
    tm = x.shape[0]
    lane = lax.broadcasted_iota(jnp.int32, (tm, LANES), 1)
    first_half = (lane % HEAD_DIM) < (HEAD_DIM // 2)

    def proj(name):
        o, w = W_OFF[name]
        return _dot(h, w_ref[:, o:o + w])

    def rope(y):
        partner = jnp.where(first_half, pltpu.roll(y, LANES - HEAD_DIM // 2, 1),
                            pltpu.roll(y, HEAD_DIM // 2, 1))
        return y * cos + partner * sin

    for name, ref in (('a_qw', qa_ref), ('b_qw', qb_ref)):
        y = proj(name)
        for c in range(N_HEADS):
            ref[:, c * LANES:(c + 1) * LANES] = (
                rope(y[:, c * LANES:(c + 1) * LANES]) * QK_SCALE).astype(BF16)
    y = proj('a_iqw')
    for c in range(IDX_HEADS):
        qi_ref[:, c * LANES:(c + 1) * LANES] = rope(y[:, c * LANES:(c + 1) * LANES]).astype(BF16)
    y = proj('c_qw')
    qc_ref[...] = (y * QK_SCALE).astype(BF16)

    for kname, vname, f32_ref, bf_out in (('a_k', 'a_v', akv_ref, kva_ref),
                                          ('b_k', 'b_v', bkv_ref, kvb_ref),
                                          ('c_k', 'c_v', ckv_ref, kvc_ref)):
        k = proj(kname)
        if kname != 'c_k':
            k = rope(k)
        v = proj(vname)
        f32_ref[:, 0:LANES] = k
        f32_ref[:, LANES:2 * LANES] = v
        bf_out[:, 0:LANES] = k.astype(BF16)
        bf_out[:, LANES:2 * LANES] = v.astype(BF16)

    ik = rope(proj('a_ik'))
    aidx_ref[...] = ik[:, 0:IDX_DIM]
    kidx_ref[...] = ik.astype(BF16)

    for i, name in enumerate(('a_z', 'b_z', 'c_z')):
        z = proj(name)
        sz_ref[:, i * BRANCH_W:(i + 1) * BRANCH_W] = (z * jax.nn.sigmoid(z)).astype(BF16)

    y = proj('misc')
    t = y + bf_ref[...]
    logf = jnp.minimum(t, 0.0) - jnp.log1p(jnp.exp(-jnp.abs(t)))
    misc = jnp.where(lane < MISC_W, logf, y * IDX_SCALE)
    misc_ref[...] = misc
    logf_ref[...] = misc[:, MISC_F:MISC_F + N_HEADS]

    y = proj('gates')
    gate_ref[...] = jax.nn.sigmoid(y).astype(BF16)


def _proj_call(x, g, w, cos, sin, bf_pad, tm):
    n = x.shape[0]
    row = lambda width: pl.BlockSpec((tm, width), lambda i: (i, 0))
    const = lambda shape: pl.BlockSpec(shape, lambda i: (0, 0))
    out_shapes = [
        ((n, N_HEADS * LANES), BF16),
        ((n, N_HEADS * LANES), BF16),
        ((n, IDX_HEADS * LANES), BF16),
        ((n, N_HEADS * LANES), BF16),
        ((n, 2 * KV_W), BF16),
        ((n, 2 * KV_W), BF16),
        ((n, 2 * KV_W), BF16),
        ((n, LANES), BF16),
        ((n, 2 * KV_W), F32),
        ((n, IDX_DIM), F32),
        ((n, 2 * KV_W), F32),
        ((n, 2 * KV_W), F32),
        ((n, N_HEADS), F32),
        ((n, LANES), F32),
        ((n, N_BRANCH * BRANCH_W), BF16),
        ((n, N_BRANCH * D_MODEL), BF16),
    ]
    return pl.pallas_call(
        _proj_kernel,
        grid=(n // tm,),
        in_specs=[row(D_MODEL), const((1, D_MODEL)),
                  pl.BlockSpec((D_MODEL, W_TOTAL), lambda i: (0, 0), pipeline_mode=pl.Buffered(1)),
                  row(LANES), row(LANES), const((1, LANES))],
        out_specs=[row(s[1]) for s, _ in out_shapes],
        out_shape=[jax.ShapeDtypeStruct(s, d) for s, d in out_shapes],
        compiler_params=pltpu.CompilerParams(dimension_semantics=("arbitrary",),
                                             vmem_limit_bytes=VMEM_LIMIT),
        name="proj",
    )(x, g, w, cos, sin, bf_pad)


def _rows8(x):
    return x.reshape(x.shape[0] // SUBLANES, SUBLANES, x.shape[1])


def _allreduce8(x8, op):
    for shift in (4, 2, 1):
        x8 = op(x8, pltpu.roll(x8, shift, 0))
    return x8


def _init_state_t(m_sc, acc_sc):
    m_sc[...] = jnp.full(m_sc.shape, NEG_INF, F32)
    acc_sc[...] = jnp.zeros(acc_sc.shape, F32)


CHAIN = 8
N_CHAINS = N_HEADS // CHAIN


def _update_t(c, s, vt_tile, m_sc, acc_sc):
    s3 = _rows8(s)
    m_prev = m_sc[c]
    m_new = jnp.maximum(m_prev, _allreduce8(jnp.max(s3, axis=0), jnp.maximum))
    m_safe = jnp.where(m_new == NEG_INF, 0.0, m_new)
    p = jnp.exp2(s3 - m_safe[None]).reshape(s.shape).astype(BF16)
    alpha = jnp.exp2(m_prev - m_safe)
    acc = acc_sc[c]
    per_group = GROUP * (s.shape[1] // CHAIN)
    first = (c * CHAIN) // GROUP
    pv = [_dot(vt_tile[first + i], p[:, i * per_group:(i + 1) * per_group])
          for i in range(max(1, CHAIN // GROUP))]
    pv = pv[0] if len(pv) == 1 else jnp.concatenate(pv, axis=1)
    acc_sc[c] = (_rows8(acc) * alpha[None]).reshape(acc.shape) + pv
    m_sc[c] = m_new


def _group_queries(qg_sc, head_block, tq):
    for h in range(N_HEADS):
        c, j = divmod(h, CHAIN)
        qg_sc[c, :, j * tq:(j + 1) * tq] = head_block(h)


def _per_head(x):
    return jnp.concatenate([x] * CHAIN, axis=1)


def _write_t(o_ref, acc_sc, tq):
    for h in range(N_HEADS):
        c, j = divmod(h, CHAIN)
        acc = acc_sc[c, :, j * tq:(j + 1) * tq]
        o_ref[h * HEAD_DIM:(h + 1) * HEAD_DIM, :] = (
            acc[0:HEAD_DIM] / acc[VT_ONE:VT_ONE + 1]).astype(o_ref.dtype)


def _for_tiles(n, tile):
    def body(i, c):
        tile(4 * i)
        tile(4 * i + 1)
        tile(4 * i + 2)
        tile(4 * i + 3)
        return c

    lax.fori_loop(0, n // 4, body, 0)

    def rest(i, c):
        tile(i)
        return c

    lax.fori_loop((n // 4) * 4, n, rest, 0)


def _causal_bias_t(tk, tq):
    key = lax.broadcasted_iota(jnp.int32, (tk, tq), 0)
    qry = lax.broadcasted_iota(jnp.int32, (tk, tq), 1)
    return jnp.where(key <= qry, 0.0, NEG_INF).astype(F32)


def _attn_scratch(tq, dk):
    return [pltpu.VMEM((N_CHAINS, dk, CHAIN * tq), BF16),
            pltpu.VMEM((N_CHAINS, SUBLANES, CHAIN * tq), F32),
            pltpu.VMEM((N_CHAINS, VT_ROWS, CHAIN * tq), F32)]


def _prompt_params():
    return pltpu.CompilerParams(dimension_semantics=("arbitrary", "arbitrary"),
                                vmem_limit_bytes=VMEM_LIMIT)


def _fox_aug_mats():
    pk = np.zeros((3, LANES, LANES), np.float32)
    pq = np.zeros((3, LANES, N_HEADS * LANES), np.float32)
    kconst = np.zeros((1, LANES), np.float32)
    qconst = np.zeros((1, N_HEADS * LANES), np.float32)
    for t in range(3):
        kconst[0, AUG_ONE + t] = 1.0
        for h in range(N_HEADS):
            pk[t, MISC_F + h, AUG_C + 3 * h + t] = -1.0
            pq[t, MISC_F + h, h * LANES + AUG_ONE + t] = 1.0
            qconst[0, h * LANES + AUG_C + 3 * h + t] = 1.0
    return pk, pq, kconst, qconst


def _foxprep_kernel(misc_ref, kv_ref, ltri_ref, pk_ref, pq_ref, kconst_ref, qconst_ref,
                    kaug_ref, qaug_ref, carry_sc):
    @pl.when(pl.program_id(1) == 0)
    def _():
        carry_sc[...] = jnp.zeros(carry_sc.shape, F32)

    lf = misc_ref[...]
    rows = lf.shape[0]
    c = _dot3_left(ltri_ref[...], lf) + carry_sc[0:1, :]
    carry_sc[0:1, :] = c[rows - 1:rows, :]
    ka = kconst_ref[...]
    qa = qconst_ref[...]
    for t, part in enumerate(_split3(c * LOG2E)):
        ka = ka + _dot(part, pk_ref[t])
        qa = qa + _dot(part, pq_ref[t])
    kaug_ref[:, 0:LANES] = kv_ref[...]
    kaug_ref[:, LANES:2 * LANES] = ka.astype(BF16)
    qaug_ref[...] = qa.astype(BF16)


def _foxprep_call(misc, kvc, batch, t_len, chunk):
    n = misc.shape[0]
    nch = t_len // chunk
    pk, pq, kconst, qconst = _fox_aug_mats()
    ltri = np.tril(np.ones((chunk, chunk), np.float32))
    rowblk = lambda width, col=0: pl.BlockSpec((chunk, width), lambda b, i: (b * nch + i, col))
    const = lambda shape: pl.BlockSpec(shape, lambda b, i: tuple(0 for _ in shape))
    return pl.pallas_call(
        _foxprep_kernel,
        grid=(batch, nch),
        in_specs=[rowblk(LANES), rowblk(LANES), const((chunk, chunk)), const((3, LANES, LANES)),
                  const((3, LANES, N_HEADS * LANES)), const((1, LANES)), const((1, N_HEADS * LANES))],
        out_specs=[rowblk(2 * LANES), rowblk(N_HEADS * LANES)],
        out_shape=[jax.ShapeDtypeStruct((n, 2 * LANES), BF16),
                   jax.ShapeDtypeStruct((n, N_HEADS * LANES), BF16)],
        scratch_shapes=[pltpu.VMEM((SUBLANES, LANES), F32)],
        compiler_params=_prompt_params(),
        name="foxprep",
    )(misc, kvc, jnp.asarray(ltri, BF16), jnp.asarray(pk, BF16), jnp.asarray(pq, BF16),
      jnp.asarray(kconst, F32), jnp.asarray(qconst, F32))


def _fox_prompt_kernel(qt_ref, kaug_ref, vt_ref, o_ref, qg_sc, m_sc, acc_sc, *, tq, tk):
    qi = pl.program_id(1)
    _init_state_t(m_sc, acc_sc)
    _group_queries(qg_sc, lambda h: qt_ref[h], tq)

    def tile(ki, bias):
        kt = kaug_ref[pl.ds(pl.multiple_of(ki * tk, tk), tk), :]
        for c in range(N_CHAINS):
            s = _dot(kt, qg_sc[c])
            if bias is not None:
                s = s + bias
            _update_t(c, s, vt_ref.at[ki], m_sc, acc_sc)

    _for_tiles(qi, lambda ki: tile(ki, None))
    tile(qi, _per_head(_causal_bias_t(tk, tq)))
    _write_t(o_ref, acc_sc, tq)


def _vt_spec(nk, tk):
    return pl.BlockSpec((nk, N_KV_HEADS, VT_ROWS, tk), lambda b, i: (b, 0, 0, 0))


def _fox_prompt_call(qpt, kaug, vt, batch, t_len, tq):
    n = kaug.shape[0]
    nq = t_len // tq
    return pl.pallas_call(
        functools.partial(_fox_prompt_kernel, tq=tq, tk=tq),
        grid=(batch, nq),
        in_specs=[pl.BlockSpec((N_HEADS, 2 * LANES, tq), lambda b, i: (0, 0, b * nq + i)),
                  pl.BlockSpec((t_len, 2 * LANES), lambda b, i: (b, 0)),
                  _vt_spec(nq, tq)],
        out_specs=pl.BlockSpec((BRANCH_W, tq), lambda b, i: (0, b * nq + i)),
        out_shape=jax.ShapeDtypeStruct((BRANCH_W, n), BF16),
        scratch_shapes=_attn_scratch(tq, 2 * LANES),
        compiler_params=_prompt_params(),
        name="fox_prompt",
    )(qpt, kaug, vt)


def _sortable_key(score):
    bits = lax.bitcast_convert_type(score + 0.0, jnp.int32)
    return jnp.where(bits < 0, bits ^ jnp.int32(0x7FFFFFFF), bits)


def _selected(key, idx, thr, cutoff):
    sel = jnp.logical_or(key > thr, jnp.logical_and(key == thr, idx <= cutoff))
    return jnp.logical_and(sel, key != INT_MIN)


COUNT_UNKNOWN = float(2 ** 30)


def _kth_largest(count_ge, shape, k):
    c0 = count_ge(jnp.zeros(shape, jnp.int32))
    c_pos = count_ge(jnp.ones(shape, jnp.int32))
    zero_tie = jnp.logical_and(c0 >= k, c_pos < k)
    ans = jnp.where(c0 >= k, 0, INT_MIN).astype(jnp.int32)
    cnt = jnp.where(c0 >= k, c0, COUNT_UNKNOWN)

    def is_open(cnt):
        return (jnp.max(jnp.where(zero_tie, k, cnt)) > k).astype(jnp.int32)

    def step(i, state):
        ans, cnt = state
        cand = ans + lax.shift_left(jnp.int32(1), jnp.int32(30) - i)
        c = count_ge(cand)
        take = c >= k
        return jnp.where(take, cand, ans), jnp.where(take, c, cnt)

    n_blind = 15
    ans, cnt = lax.fori_loop(0, n_blind, step, (ans, cnt))

    def cond(state):
        i, _, _, still_open = state
        return jnp.logical_and(i < 31, still_open > 0)

    def body(state):
        i, ans, cnt, _ = state
        ans, cnt = step(i + 1, step(i, (ans, cnt)))
        return i + 2, ans, cnt, is_open(cnt)

    _, ans, cnt, _ = lax.while_loop(cond, body, (jnp.int32(n_blind), ans, cnt, is_open(cnt)))
    return ans, cnt, c_pos, zero_tie


def _tie_cutoff(count_tie_lt, need, shape, nbits):
    def body(i, x):
        cand = x + lax.shift_left(jnp.int32(1), jnp.int32(nbits - 1) - i)
        return jnp.where(count_tie_lt(cand) < need, cand, x)

    return lax.fori_loop(0, nbits, body, jnp.zeros(shape, jnp.int32))


def _dsa_prompt_kernel(qt_ref, qit_ref, misct_ref, kidx_ref, k_ref, vt_ref, lstrict_ref, o_ref,
                       s_sc, before_sc, qig_sc, qg_sc, m_sc, acc_sc, *, tq, tk, n_sel):
    qi = pl.program_id(1)
    nk = qi + 1
    key_row = lax.broadcasted_iota(jnp.int32, (tk, tq), 0)
    qry_col = lax.broadcasted_iota(jnp.int32, (tk, tq), 1)
    row3 = _rows8(key_row)
    for h in range(IDX_HEADS):
        qig_sc[:, h * tq:(h + 1) * tq] = qit_ref[h * LANES:(h + 1) * LANES, :]
    wrow = jnp.concatenate([misct_ref[MISC_W + h:MISC_W + h + 1, :] for h in range(IDX_HEADS)], axis=1)

    def score_tile(ki):
        kt = kidx_ref[pl.ds(pl.multiple_of(ki * tk, tk), tk), :]
        r = jnp.maximum(_dot(kt, qig_sc[...]), 0.0) * wrow
        acc = r[:, 0:tq]
        for h in range(1, IDX_HEADS):
            acc = acc + r[:, h * tq:(h + 1) * tq]
        return _sortable_key(acc)

    def fill(ki):
        s_sc[ki] = score_tile(ki)

    _for_tiles(qi, fill)
    s_sc[qi] = jnp.where(key_row <= qry_col, score_tile(qi), INT_MIN)

    def counter(pred):
        def count(arg):
            def one(ki, acc):
                hit = pred(_rows8(s_sc[ki]), ki * tk + row3, arg[None])
                return acc + jnp.sum(jnp.where(hit, 1.0, 0.0), axis=0)

            acc = lax.fori_loop(0, nk // 2, lambda i, a: one(2 * i + 1, one(2 * i, a)),
                                jnp.zeros((SUBLANES, tq), F32))
            acc = lax.cond(nk % 2 == 1, lambda a: one(nk - 1, a), lambda a: a, acc)
            return _allreduce8(acc, jnp.add)
        return count

    kf = float(n_sel)
    stat = (SUBLANES, tq)
    thr, cnt_ge, c_pos, zero_tie = _kth_largest(counter(lambda key, idx, x: key >= x), stat, kf)

    general = jnp.logical_and(jnp.logical_and(cnt_ge > kf, jnp.logical_not(zero_tie)), thr != INT_MIN)
    need_general = lax.cond(
        jnp.max(jnp.where(general, 1.0, 0.0)) > 0.0,
        lambda: kf - counter(lambda key, idx, x: key > x)(thr),
        lambda: jnp.zeros(stat, F32))
    need = jnp.where(zero_tie, kf - c_pos, jnp.where(general, need_general, COUNT_UNKNOWN))

    def ties_before(ki, run):
        before_sc[ki] = run
        tie = _rows8(s_sc[ki]) == thr[None]
        return run + _allreduce8(jnp.sum(jnp.where(tie, 1.0, 0.0), axis=0), jnp.add)

    lax.fori_loop(0, nk, ties_before, jnp.zeros(stat, F32))

    def to_bias(ki):
        key = _rows8(s_sc[ki])
        tie = key == thr[None]
        earlier = _dot(lstrict_ref[...], jnp.where(tie, 1.0, 0.0).reshape(tk, tq).astype(BF16))
        rank = _rows8(earlier) + before_sc[ki][None]
        sel = jnp.logical_or(key > thr[None], jnp.logical_and(tie, rank < need[None]))
        sel = jnp.logical_and(sel, key != INT_MIN)
        bias = jnp.where(sel, 0.0, NEG_INF).astype(F32).reshape(tk, tq)
        s_sc[ki] = lax.bitcast_convert_type(bias, jnp.int32)

    _for_tiles(nk, to_bias)

    _init_state_t(m_sc, acc_sc)
    _group_queries(qg_sc, lambda h: qt_ref[h * LANES:(h + 1) * LANES, :], tq)

    def attend(ki):
        bias = _per_head(lax.bitcast_convert_type(s_sc[ki], F32))
        kt = k_ref[pl.ds(pl.multiple_of(ki * tk, tk), tk), :]
        for c in range(N_CHAINS):
            _update_t(c, _dot(kt, qg_sc[c]) + bias, vt_ref.at[ki], m_sc, acc_sc)

    _for_tiles(nk, attend)
    _write_t(o_ref, acc_sc, tq)


def _dsa_prompt_call(qat, qit, misct, kidx, kva, vt, batch, t_len, tq):
    n = kidx.shape[0]
    nq = t_len // tq
    n_sel = min(DSA_TOPK, t_len // 4)
    colblk = lambda rows: pl.BlockSpec((rows, tq), lambda b, i: (0, b * nq + i))
    seq = lambda width: pl.BlockSpec((t_len, width), lambda b, i: (b, 0))
    lstrict = np.tril(np.ones((tq, tq), np.float32), -1)
    return pl.pallas_call(
        functools.partial(_dsa_prompt_kernel, tq=tq, tk=tq, n_sel=n_sel),
        grid=(batch, nq),
        in_specs=[colblk(N_HEADS * LANES), colblk(IDX_HEADS * LANES), colblk(LANES),
                  seq(LANES), seq(LANES), _vt_spec(nq, tq),
                  pl.BlockSpec((tq, tq), lambda b, i: (0, 0))],
        out_specs=colblk(BRANCH_W),
        out_shape=jax.ShapeDtypeStruct((BRANCH_W, n), BF16),
        scratch_shapes=[pltpu.VMEM((nq, tq, tq), jnp.int32),
                        pltpu.VMEM((nq, SUBLANES, tq), F32),
                        pltpu.VMEM((LANES, IDX_HEADS * tq), BF16)] + _attn_scratch(tq, LANES),
        compiler_params=_prompt_params(),
        name="dsa_prompt",
    )(qat, qit, misct, kidx, kva, vt, jnp.asarray(lstrict, BF16))


def _kmean_kernel(k_ref, o_ref):
    o_ref[...] = jnp.mean(k_ref[...], axis=0, keepdims=True)


def _kmean_call(kv_f32, n_blocks):
    return pl.pallas_call(
        _kmean_kernel,
        grid=(n_blocks,),
        in_specs=[pl.BlockSpec((MOBA_BLOCK, LANES), lambda i: (i, 0))],
        out_specs=pl.BlockSpec((None, 1, LANES), lambda i: (i, 0, 0)),
        out_shape=jax.ShapeDtypeStruct((n_blocks, 1, LANES), F32),
        name="moba_kmean",
    )(kv_f32)


def _colmax(x):
    return _allreduce8(jnp.max(_rows8(x), axis=0), jnp.maximum)


def _top_blocks_bias_t(gate, n_valid_lt, n_sel):
    blk_i = lax.broadcasted_iota(jnp.int32, gate.shape, 0)
    blk = blk_i.astype(F32)
    cur3 = _rows8(jnp.where(blk_i < n_valid_lt, gate, NEG_INF))
    blk3 = _rows8(blk)
    bias3 = jnp.full(cur3.shape, NEG_INF, F32)
    for _ in range(n_sel):
        mx = _allreduce8(jnp.max(cur3, axis=0), jnp.maximum)
        first = _allreduce8(jnp.min(jnp.where(cur3 == mx[None], blk3, float(LANES)), axis=0),
                            jnp.minimum)
        pick = jnp.logical_and(mx[None] > NEG_INF, blk3 == first[None])
        bias3 = jnp.where(pick, 0.0, bias3)
        cur3 = jnp.where(pick, NEG_INF, cur3)
    return bias3.reshape(gate.shape)


def _moba_prompt_kernel(qt_ref, kmean_ref, k_ref, vt_ref, o_ref, selb_sc, qg_sc, m_sc, acc_sc,
                        *, tq, n_sel):
    qi = pl.program_id(1)
    tk = tq
    width = CHAIN * tq
    n_blocks = selb_sc.shape[1]
    km = kmean_ref[...]
    _group_queries(qg_sc, lambda h: qt_ref[h * LANES:(h + 1) * LANES, :], tq)
    for c in range(N_CHAINS):
        bias = _top_blocks_bias_t(_dot(km, qg_sc[c]), qi, n_sel)
        for b in range(n_blocks):
            selb_sc[c, b] = jnp.broadcast_to(bias[b:b + 1, :], (SUBLANES, width))
    _init_state_t(m_sc, acc_sc)

    def tile(ki, causal):
        kt = k_ref[pl.ds(pl.multiple_of(ki * tk, tk), tk), :]
        for c in range(N_CHAINS):
            s = _dot(kt, qg_sc[c])
            if causal is not None:
                s = s + causal
            else:
                s = (_rows8(s) + selb_sc[c, ki][None]).reshape(tk, width)
            _update_t(c, s, vt_ref.at[ki], m_sc, acc_sc)

    tile(qi, _per_head(_causal_bias_t(tk, tq)))
    _for_tiles(qi, lambda ki: tile(ki, None))
    _write_t(o_ref, acc_sc, tq)


def _moba_prompt_call(qbt, kmean, kvb, vt, batch, t_len):
    n = kvb.shape[0]
    tq = MOBA_BLOCK
    nq = t_len // tq
    n_sel = min(MOBA_TOPK, nq)
    colblk = lambda rows: pl.BlockSpec((rows, tq), lambda b, i: (0, b * nq + i))
    return pl.pallas_call(
        functools.partial(_moba_prompt_kernel, tq=tq, n_sel=n_sel),
        grid=(batch, nq),
        in_specs=[colblk(N_HEADS * LANES),
                  pl.BlockSpec((None, LANES, LANES), lambda b, i: (b, 0, 0)),
                  pl.BlockSpec((t_len, LANES), lambda b, i: (b, 0)),
                  _vt_spec(nq, tq)],
        out_specs=colblk(BRANCH_W),
        out_shape=jax.ShapeDtypeStruct((BRANCH_W, n), BF16),
        scratch_shapes=[pltpu.VMEM((N_CHAINS, nq, SUBLANES, CHAIN * tq), F32)] + _attn_scratch(tq, LANES),
        compiler_params=_prompt_params(),
        name="moba_prompt",
    )(qbt, kmean, kvb, vt)


def _merge_kernel(x_ref, oa_ref, ob_ref, oc_ref, sz_ref, gate_ref, wbr_ref, wout_ref, gfin_ref,
                  o_ref, *, final):
    m = None
    for n, o_n in enumerate((oa_ref, ob_ref, oc_ref)):
        a = o_n[...] * sz_ref[:, n * BRANCH_W:(n + 1) * BRANCH_W]
        u = _dot(a, wbr_ref[n])
        t = gate_ref[:, n * D_MODEL:(n + 1) * D_MODEL].astype(F32) * u
        m = t if m is None else m + t
    y = x_ref[...] + _dot(m.astype(BF16), wout_ref[...])
    if final:
        ms = jnp.mean(y * y, axis=-1, keepdims=True)
        y = y * lax.rsqrt(ms + RMS_EPS) * gfin_ref[...]
    o_ref[...] = y


def _merge_call(x, oa, ob, oc, sz, gates, wbr, wout, gfin, tm, final):
    n = x.shape[0]
    row = lambda width: pl.BlockSpec((tm, width), lambda i: (i, 0))
    return pl.pallas_call(
        functools.partial(_merge_kernel, final=final),
        grid=(n // tm,),
        in_specs=[row(D_MODEL), row(BRANCH_W), row(BRANCH_W), row(BRANCH_W),
                  row(N_BRANCH * BRANCH_W), row(N_BRANCH * D_MODEL),
                  pl.BlockSpec((N_BRANCH, BRANCH_W, D_MODEL), lambda i: (0, 0, 0)),
                  pl.BlockSpec((D_MODEL, D_MODEL), lambda i: (0, 0)),
                  pl.BlockSpec((1, D_MODEL), lambda i: (0, 0))],
        out_specs=row(D_MODEL),
        out_shape=jax.ShapeDtypeStruct((n, D_MODEL), F32),
        compiler_params=pltpu.CompilerParams(dimension_semantics=("arbitrary",),
                                             vmem_limit_bytes=VMEM_LIMIT),
        name="merge",
    )(x, oa, ob, oc, sz, gates, wbr, wout, gfin)


def _expand8(x):
    return jnp.concatenate([jnp.broadcast_to(x[h:h + 1, :], (SUBLANES, x.shape[1]))
                            for h in range(x.shape[0])], axis=0)


def _tile8(x, reps):
    return jnp.concatenate([x] * reps, axis=0)


def _new_key_bias(rows, n_new):
    t8 = lax.broadcasted_iota(jnp.int32, (rows, LANES), 0) % SUBLANES
    lane = lax.broadcasted_iota(jnp.int32, (rows, LANES), 1)
    return jnp.where(lane <= jnp.minimum(t8, n_new - 1), 0.0, NEG_INF).astype(F32)


def _init_state_d(m_sc, l_sc, acc_sc):
    m_sc[...] = jnp.full(m_sc.shape, NEG_INF, F32)
    l_sc[...] = jnp.zeros(l_sc.shape, F32)
    acc_sc[...] = jnp.zeros(acc_sc.shape, F32)


def _update_d(s, pv, m_sc, l_sc, acc_sc):
    m_prev = m_sc[...]
    m_new = jnp.maximum(m_prev, jnp.max(s, axis=1, keepdims=True))
    m_safe = jnp.where(m_new == NEG_INF, 0.0, m_new)
    p = jnp.exp2(s - m_safe)
    alpha = jnp.exp2(m_prev - m_safe)
    l_sc[...] = alpha * l_sc[...] + jnp.sum(p, axis=1, keepdims=True)
    acc_sc[...] = alpha * acc_sc[...] + pv(p.astype(BF16))
    m_sc[...] = m_new


def _page_logits(q, pages):
    return _dot(q, jnp.concatenate([pg[0].astype(BF16) for pg in pages], axis=1))


def _page_pv(pages):
    def pv(p):
        acc = None
        for j, pg in enumerate(pages):
            t = _dot_nt(p[:, j * LANES:(j + 1) * LANES], pg[1].astype(BF16))
            acc = t if acc is None else acc + t
        return acc
    return pv


def _decode_state():
    rows = N_HEADS * SUBLANES
    return [pltpu.VMEM((rows, 1), F32), pltpu.VMEM((rows, 1), F32), pltpu.VMEM((rows, LANES), F32)]


def _dsa_decode_kernel(pt_ref, qi_ref, w_ref, q_ref, knew_idx_ref, knew_ref, vnew_ref, *rest,
                       n_pg, n_groups, n_sel, n_new, nbits):
    idx_pages = rest[:n_pg]
    kv_pages = rest[n_pg:2 * n_pg]
    o_ref, s_sc, thr_sc, cut_sc, m_sc, l_sc, acc_sc = rest[2 * n_pg:]
    ph = pl.program_id(1)
    g = pl.program_id(2)
    width = n_pg * LANES
    lane_w = lax.broadcasted_iota(jnp.int32, (SUBLANES, width), 1)

    def scores(r):
        r = jnp.maximum(r, 0.0) * w_ref[:, 0:1]
        acc = r[0:SUBLANES]
        for h in range(1, IDX_HEADS):
            acc = acc + r[h * SUBLANES:(h + 1) * SUBLANES]
        return _sortable_key(acc)

    @pl.when(ph == 0)
    def _():
        qi = qi_ref[:, 0:IDX_DIM]
        r = _dot(qi, jnp.concatenate([pg[...].astype(BF16) for pg in idx_pages], axis=1))
        s_sc[g] = scores(r)

    @pl.when(jnp.logical_and(ph == 0, g == n_groups - 1))
    def _():
        key = scores(_dot_nt(qi_ref[...], knew_idx_ref[...]))
        visible = _new_key_bias(SUBLANES, n_new) == 0.0
        t8 = lax.broadcasted_iota(jnp.int32, (SUBLANES, LANES), 0)
        key = jnp.where(jnp.logical_and(visible, t8 < n_new), key, INT_MIN)
        s_sc[n_groups] = jnp.full((SUBLANES, width), INT_MIN, jnp.int32)
        s_sc[n_groups, :, 0:LANES] = key

        def counter(pred):
            def count(arg):
                acc = jnp.zeros((SUBLANES, width), F32)
                for gi in range(n_groups + 1):
                    acc = acc + jnp.where(pred(s_sc[gi], gi * width + lane_w, arg), 1.0, 0.0)
                return jnp.sum(acc, axis=1, keepdims=True)
            return count

        kf = float(n_sel)
        stat = (SUBLANES, 1)
        thr, _, _, _ = _kth_largest(counter(lambda key, idx, x: key >= x), stat, kf)
        need = kf - counter(lambda key, idx, x: key > x)(thr)
        tie_lt = counter(lambda key, idx, x: jnp.logical_and(key == thr, idx < x))
        cut = _tie_cutoff(tie_lt, need, stat, nbits)
        thr_sc[...] = jnp.broadcast_to(thr, (SUBLANES, LANES))
        cut_sc[...] = jnp.broadcast_to(cut, (SUBLANES, LANES))
        _init_state_d(m_sc, l_sc, acc_sc)

    def sel_bias(key, idx):
        sel = _selected(key, idx, thr_sc[:, 0:1], cut_sc[:, 0:1])
        return _tile8(jnp.where(sel, 0.0, NEG_INF).astype(F32), N_HEADS)

    @pl.when(ph == 1)
    def _():
        s = _page_logits(q_ref[...], kv_pages) + sel_bias(s_sc[g], g * width + lane_w)
        _update_d(s, _page_pv(kv_pages), m_sc, l_sc, acc_sc)

    @pl.when(jnp.logical_and(ph == 1, g == n_groups - 1))
    def _():
        key = s_sc[n_groups, :, 0:LANES]
        s = _dot_nt(q_ref[...], knew_ref[...]) + sel_bias(key, n_groups * width + lane_w[:, 0:LANES])
        _update_d(s, lambda p: _dot(p, vnew_ref[...]), m_sc, l_sc, acc_sc)
        o_ref[...] = acc_sc[...] / l_sc[...]


def _page_specs(n_pg, block, index_of):
    return [pl.BlockSpec(block, functools.partial(index_of, j)) for j in range(n_pg)]


KV_PAGE = (None, 2, 2 * HEAD_DIM, LANES)


def _dsa_decode_call(page_table, layer_off, qi_rows, w_rows, q_rows, knew_idx, knew, vnew,
                     cache_idx, cache_kv, n_pg):
    nb, n_pages = page_table.shape
    n_groups = n_pages // n_pg
    n_new = 4
    n_sel = min(DSA_TOPK, (n_pages * LANES + n_new) // 4)
    nbits = max(1, ((n_groups + 1) * n_pg * LANES - 1).bit_length())

    def idx_map(j, b, ph, g, pt):
        gg = jnp.where(ph == 0, g, n_groups - 1)
        return (pt[b, gg * n_pg + j] + layer_off, 0, 0)

    def kv_map(j, b, ph, g, pt):
        gg = jnp.where(ph == 1, g, 0)
        return (pt[b, gg * n_pg + j] + layer_off, 0, 0, 0)

    per_b = lambda r, c: pl.BlockSpec((None, r, c), lambda b, ph, g, pt: (b, 0, 0))
    rows_q = N_HEADS * SUBLANES
    rows_i = IDX_HEADS * SUBLANES
    grid_spec = pltpu.PrefetchScalarGridSpec(
        num_scalar_prefetch=1,
        grid=(nb, 2, n_groups),
        in_specs=[per_b(rows_i, LANES), per_b(rows_i, LANES),
                  per_b(rows_q, LANES), per_b(LANES, LANES), per_b(LANES, LANES), per_b(LANES, LANES)]
        + _page_specs(n_pg, (None, IDX_DIM, LANES), idx_map)
        + _page_specs(n_pg, KV_PAGE, kv_map),
        out_specs=per_b(rows_q, LANES),
        scratch_shapes=[pltpu.VMEM((n_groups + 1, SUBLANES, n_pg * LANES), jnp.int32),
                        pltpu.VMEM((SUBLANES, LANES), jnp.int32), pltpu.VMEM((SUBLANES, LANES), jnp.int32)]
        + _decode_state(),
    )
    return pl.pallas_call(
        functools.partial(_dsa_decode_kernel, n_pg=n_pg, n_groups=n_groups, n_sel=n_sel,
                          n_new=n_new, nbits=nbits),
        grid_spec=grid_spec,
        out_shape=jax.ShapeDtypeStruct((nb, rows_q, LANES), F32),
        compiler_params=pltpu.CompilerParams(
            dimension_semantics=("arbitrary", "arbitrary", "arbitrary"), vmem_limit_bytes=VMEM_LIMIT),
        name="dsa_decode",
    )(page_table, qi_rows, w_rows, q_rows, knew_idx, knew, vnew,
      *([cache_idx] * n_pg), *([cache_kv] * n_pg))


def _moba_decode_kernel(pt_ref, q_ref, knew_ref, vnew_ref, *rest, n_pg, n_groups, n_sel, n_new):
    kv_pages = rest[:n_pg]
    o_ref, gate_sc, mb_sc, lb_sc, accb_sc, m_sc, l_sc, acc_sc = rest[n_pg:]
    g = pl.program_id(1)
    ppb = MOBA_BLOCK // LANES
    blocks_per_step = n_pg // ppb
    n_blocks = n_groups * blocks_per_step
    rows = N_HEADS * SUBLANES
    q = q_ref[...]

    for jb in range(blocks_per_step):
        pages = kv_pages[jb * ppb:(jb + 1) * ppb]
        ksum = None
        for pg in pages:
            t = jnp.sum(pg[0], axis=1, keepdims=True)
            ksum = t if ksum is None else ksum + t
        kmean = jnp.broadcast_to(ksum * (1.0 / MOBA_BLOCK), (LANES, LANES)).astype(BF16)
        gate = _dot(q, kmean)
        s = _page_logits(q, pages)
        mj = jnp.max(s, axis=1, keepdims=True)
        p = jnp.exp2(s - mj)
        b = g * blocks_per_step + jb
        gate_sc[b] = gate
        mb_sc[b] = jnp.broadcast_to(mj, (rows, LANES))
        lb_sc[b] = jnp.broadcast_to(jnp.sum(p, axis=1, keepdims=True), (rows, LANES))
        accb_sc[b] = _page_pv(pages)(p.astype(BF16))

    @pl.when(g == n_groups - 1)
    def _():
        _init_state_d(m_sc, l_sc, acc_sc)
        s = _dot_nt(q, knew_ref[...]) + _new_key_bias(rows, n_new)
        _update_d(s, lambda p: _dot(p, vnew_ref[...]), m_sc, l_sc, acc_sc)
        m_own = jnp.broadcast_to(m_sc[...], (rows, LANES))

        def pick_round(_, c):
            def mx_body(b, mx):
                return jnp.maximum(mx, gate_sc[b])
            mx = lax.fori_loop(0, n_blocks, mx_body, jnp.full((rows, LANES), NEG_INF, F32))

            def first_body(b, first):
                return jnp.minimum(first, jnp.where(gate_sc[b] == mx, b, n_blocks))
            first = lax.fori_loop(0, n_blocks, first_body, jnp.full((rows, LANES), n_blocks, jnp.int32))

            def mark_body(b, cc):
                hit = jnp.logical_and(first == b, mx > NEG_INF)
                gate_sc[b] = jnp.where(hit, NEG_INF, gate_sc[b])
                lb_sc[b] = jnp.where(hit, -lb_sc[b], lb_sc[b])
                return cc
            lax.fori_loop(0, n_blocks, mark_body, 0)
            return c

        lax.fori_loop(0, n_sel, pick_round, 0)

        def max_body(b, mt):
            return jnp.maximum(mt, jnp.where(lb_sc[b] < 0.0, mb_sc[b], NEG_INF))
        m_tot = lax.fori_loop(0, n_blocks, max_body, m_own)

        def comb_body(b, carry):
            l_tot, acc_tot = carry
            picked = lb_sc[b] < 0.0
            wgt = jnp.where(picked, jnp.exp2(mb_sc[b] - m_tot), 0.0)
            return l_tot - wgt * lb_sc[b], acc_tot + wgt * accb_sc[b]

        w_own = jnp.exp2(m_own - m_tot)
        l0 = w_own * jnp.broadcast_to(l_sc[...], (rows, LANES))
        a0 = w_own * acc_sc[...]
        l_tot, acc_tot = lax.fori_loop(0, n_blocks, comb_body, (l0, a0))
        o_ref[...] = acc_tot / l_tot


def _moba_decode_call(page_table, layer_off, q_rows, knew, vnew, cache_kv, n_pg):
    nb, n_pages = page_table.shape
    n_groups = n_pages // n_pg
    n_blocks = n_pages * LANES // MOBA_BLOCK
    n_sel = min(MOBA_TOPK, n_blocks + 1)
    rows_q = N_HEADS * SUBLANES

    def kv_map(j, b, g, pt):
        return (pt[b, g * n_pg + j] + layer_off, 0, 0, 0)

    per_b = lambda r, c: pl.BlockSpec((None, r, c), lambda b, g, pt: (b, 0, 0))
    slab = lambda: pltpu.VMEM((n_blocks, rows_q, LANES), F32)
    grid_spec = pltpu.PrefetchScalarGridSpec(
        num_scalar_prefetch=1,
        grid=(nb, n_groups),
        in_specs=[per_b(rows_q, LANES), per_b(LANES, LANES), per_b(LANES, LANES)]
        + _page_specs(n_pg, KV_PAGE, kv_map),
        out_specs=per_b(rows_q, LANES),
        scratch_shapes=[slab(), slab(), slab(), slab()] + _decode_state(),
    )
    return pl.pallas_call(
        functools.partial(_moba_decode_kernel, n_pg=n_pg, n_groups=n_groups, n_sel=n_sel, n_new=4),
        grid_spec=grid_spec,
        out_shape=jax.ShapeDtypeStruct((nb, rows_q, LANES), F32),
        compiler_params=pltpu.CompilerParams(dimension_semantics=("arbitrary", "arbitrary"),
                                             vmem_limit_bytes=VMEM_LIMIT),
        name="moba_decode",
    )(page_table, q_rows, knew, vnew, *([cache_kv] * n_pg))


def _fox_decode_kernel(pt_ref, q_ref, knew_ref, vnew_ref, lfnew_rows_ref, lfnew_t_ref,
                       tri_ref, upper_ref, sfx_ref, *rest, n_pg, n_groups, n_new):
    kv_pages = rest[:n_pg]
    lf_pages = rest[n_pg:2 * n_pg]
    o_ref, cq_sc, carry_sc, m_sc, l_sc, acc_sc = rest[2 * n_pg:]
    g = pl.program_id(1)
    rows = N_HEADS * SUBLANES
    q = q_ref[...]

    @pl.when(g == 0)
    def _():
        cq_rows = _dot3_left(tri_ref[...], lfnew_rows_ref[...]) * LOG2E
        cq_sc[...] = cq_rows
        new_cum_t = _dot3(lfnew_t_ref[...], upper_ref[...]) * LOG2E
        carry_sc[...] = jnp.zeros(carry_sc.shape, F32)
        _init_state_d(m_sc, l_sc, acc_sc)
        s = (_dot_nt(q, knew_ref[...]) + cq_rows - _expand8(new_cum_t)
             + _new_key_bias(rows, n_new))
        _update_d(s, lambda p: _dot(p, vnew_ref[...]), m_sc, l_sc, acc_sc)

    carry = carry_sc[...]
    suffix = [None] * n_pg
    for j in range(n_pg - 1, -1, -1):
        lft = lf_pages[j][...]
        suffix[j] = (_dot3(lft, sfx_ref[...]) + carry) * LOG2E
        carry = carry + jnp.sum(lft, axis=1, keepdims=True)
    carry_sc[...] = carry
    bias = jnp.concatenate([_expand8(sf) for sf in suffix], axis=1)
    s = _page_logits(q, kv_pages) + bias + jnp.concatenate([cq_sc[...]] * n_pg, axis=1)
    _update_d(s, _page_pv(kv_pages), m_sc, l_sc, acc_sc)

    @pl.when(g == n_groups - 1)
    def _():
        o_ref[...] = acc_sc[...] / l_sc[...]


def _fox_decode_call(page_table, layer_off, q_rows, knew, vnew, lfnew_rows, lfnew_t,
                     cache_kv, cache_lft, n_pg):
    nb, n_pages = page_table.shape
    n_groups = n_pages // n_pg
    rows_q = N_HEADS * SUBLANES
    r = np.arange(rows_q)
    tri = ((r[:, None] // SUBLANES == r[None, :] // SUBLANES)
           & (r[None, :] % SUBLANES <= r[:, None] % SUBLANES)).astype(np.float32)
    lane = np.arange(LANES)
    upper = (lane[:, None] <= lane[None, :]).astype(np.float32)
    sfx = (lane[:, None] > lane[None, :]).astype(np.float32)

    def kv_map(j, b, g, pt):
        return (pt[b, (n_groups - 1 - g) * n_pg + j] + layer_off, 0, 0, 0)

    def lf_map(j, b, g, pt):
        return (pt[b, (n_groups - 1 - g) * n_pg + j] + layer_off, 0, 0)

    per_b = lambda rr, c: pl.BlockSpec((None, rr, c), lambda b, g, pt: (b, 0, 0))
    const = lambda rr, c: pl.BlockSpec((rr, c), lambda b, g, pt: (0, 0))
    grid_spec = pltpu.PrefetchScalarGridSpec(
        num_scalar_prefetch=1,
        grid=(nb, n_groups),
        in_specs=[per_b(rows_q, LANES), per_b(LANES, LANES), per_b(LANES, LANES),
                  per_b(rows_q, LANES), per_b(SUBLANES, LANES),
                  const(rows_q, rows_q), const(LANES, LANES), const(LANES, LANES)]
        + _page_specs(n_pg, KV_PAGE, kv_map)
        + _page_specs(n_pg, (None, SUBLANES, LANES), lf_map),
        out_specs=per_b(rows_q, LANES),
        scratch_shapes=[pltpu.VMEM((rows_q, LANES), F32), pltpu.VMEM((SUBLANES, LANES), F32)]
        + _decode_state(),
    )
    return pl.pallas_call(
        functools.partial(_fox_decode_kernel, n_pg=n_pg, n_groups=n_groups, n_new=4),
        grid_spec=grid_spec,
        out_shape=jax.ShapeDtypeStruct((nb, rows_q, LANES), F32),
        compiler_params=pltpu.CompilerParams(dimension_semantics=("arbitrary", "arbitrary"),
                                             vmem_limit_bytes=VMEM_LIMIT),
        name="fox_decode",
    )(page_table, q_rows, knew, vnew, lfnew_rows, lfnew_t,
      jnp.asarray(tri, BF16), jnp.asarray(upper, BF16), jnp.asarray(sfx, BF16),
      *([cache_kv] * n_pg), *([cache_lft] * n_pg))


def _rope_tables(pos):
    half = HEAD_DIM // 2
    freqs = ROPE_THETA ** (-jnp.arange(half, dtype=F32) / half)
    ang = pos.astype(F32)[:, None] * freqs[None, :]
    cos = jnp.tile(jnp.cos(ang), (1, LANES // half))
    sin = jnp.sin(ang)
    sin = jnp.tile(jnp.concatenate([-sin, sin], axis=1), (1, LANES // HEAD_DIM))
    return cos, sin


def _layout_w_in(w_in_l):
    cols = jnp.asarray(np.maximum(W_SRC, 0), jnp.int32)
    keep = jnp.asarray(W_SRC >= 0)
    return jnp.where(keep[None, :], jnp.take(w_in_l, cols, axis=1), 0.0).astype(BF16)


def _decode_rows(a, nb, t_new, heads):
    a = a.reshape(nb, t_new, heads, LANES).transpose(0, 2, 1, 3)
    a = jnp.pad(a, ((0, 0), (0, 0), (0, SUBLANES - t_new), (0, 0)))
    return a.reshape(nb, heads * SUBLANES, LANES)


def _new_rows(a, nb, t_new):
    a = a.reshape(nb, t_new, LANES)
    return jnp.pad(a, ((0, 0), (0, LANES - t_new), (0, 0)))


def _pick_pages_per_step(n_pages):
    for c in (32, 16, 8, 4, 2):
        if n_pages % c == 0:
            return c
    raise ValueError("page count must be even")


def _value_tiles(kv, tk):
    n = kv.shape[0]
    v = kv[:, KV_W:2 * KV_W].reshape(n // tk, tk, N_KV_HEADS, HEAD_DIM).transpose(0, 2, 3, 1)
    tail = jnp.zeros((n // tk, N_KV_HEADS, VT_ROWS - HEAD_DIM, tk), BF16).at[:, :, 0, :].set(1.0)
    return jnp.concatenate([v, tail], axis=2)


def _prompt_branches(proj, batch, t_len):
    (qa, qb, qi, qc, kva, kvb, kvc, kidx, _, _, bkv, _, _, misc, _, _) = proj
    tq = MOBA_BLOCK
    n = qa.shape[0]
    n_blocks = t_len // MOBA_BLOCK
    o_a = _dsa_prompt_call(qa.T, qi.T, misc.T, kidx, kva, _value_tiles(kva, tq), batch, t_len, tq)
    kmean = _kmean_call(bkv, batch * n_blocks).reshape(batch, n_blocks, LANES)
    kmean = jnp.pad(kmean, ((0, 0), (0, LANES - n_blocks), (0, 0))).astype(BF16)
    o_b = _moba_prompt_call(qb.T, kmean, kvb, _value_tiles(kvb, tq), batch, t_len)
    kaug, qaug = _foxprep_call(misc, kvc, batch, t_len, LANES)
    qpt = jnp.concatenate([qc.T.reshape(N_HEADS, LANES, n), qaug.T.reshape(N_HEADS, LANES, n)], axis=1)
    o_c = _fox_prompt_call(qpt, kaug, _value_tiles(kvc, tq), batch, t_len, tq)
    return o_a.T, o_b.T, o_c.T


def _sample_branches(proj, nb, t_new, page_table, layer_off, caches):
    (qa, qb, qi, qc, kva, kvb, kvc, kidx, _, _, _, _, logf, misc, _, _) = proj
    ca_idx, ca_kv, cb_kv, cc_kv, cc_lft = caches
    n_pg = _pick_pages_per_step(page_table.shape[1])

    def head_rows(col, heads):
        a = col.reshape(nb, t_new, heads).transpose(0, 2, 1)
        a = jnp.pad(a, ((0, 0), (0, 0), (0, SUBLANES - t_new)))
        return jnp.broadcast_to(a.reshape(nb, heads * SUBLANES, 1), (nb, heads * SUBLANES, LANES))

    def new_kv(kv):
        return _new_rows(kv[:, 0:LANES], nb, t_new), _new_rows(kv[:, LANES:2 * LANES], nb, t_new)

    d_a = _dsa_decode_call(
        page_table, layer_off, _decode_rows(qi, nb, t_new, IDX_HEADS),
        head_rows(misc[:, MISC_W:MISC_W + IDX_HEADS], IDX_HEADS),
        _decode_rows(qa, nb, t_new, N_HEADS), _new_rows(kidx, nb, t_new), *new_kv(kva),
        ca_idx, ca_kv, n_pg)
    d_b = _moba_decode_call(page_table, layer_off, _decode_rows(qb, nb, t_new, N_HEADS),
                            *new_kv(kvb), cb_kv, n_pg)
    lf_t = jnp.pad(logf.reshape(nb, t_new, N_HEADS).transpose(0, 2, 1),
                   ((0, 0), (0, 0), (0, LANES - t_new)))
    d_c = _fox_decode_call(page_table, layer_off, _decode_rows(qc, nb, t_new, N_HEADS),
                           *new_kv(kvc), head_rows(logf, N_HEADS), lf_t, cc_kv, cc_lft, n_pg)

    def compact(d):
        d = d.reshape(nb, N_KV_HEADS, GROUP, SUBLANES, N_KV_HEADS, HEAD_DIM)[:, :, :, :t_new]
        d = jnp.stack([d[:, gidx, :, :, gidx, :] for gidx in range(N_KV_HEADS)], axis=1)
        return d.transpose(0, 3, 1, 2, 4).reshape(nb * t_new, BRANCH_W).astype(BF16)

    return compact(d_a), compact(d_b), compact(d_c)


def _native_pages(cache, depth, n_pool):
    nd = cache.ndim
    c = jnp.transpose(cache, (0, 1) + tuple(range(3, nd)) + (2,))
    return c.reshape((depth * n_pool,) + c.shape[2:])


def kernel(x_prompt, x_sample, cache_a_kv, cache_a_idx, cache_b_kv, cache_c_kv, cache_c_logf,
           page_table, norm_g, w_in, b_forget, w_branch, w_out, final_norm_g):
    batch, t_len, _ = x_prompt.shape
    nb, t_new, _ = x_sample.shape
    depth = norm_g.shape[0]
    n_pool, page_size = cache_a_kv.shape[1], cache_a_kv.shape[2]
    n_pages = page_table.shape[1]
    past_len = n_pages * page_size
    assert page_size == LANES and t_len % MOBA_BLOCK == 0 and past_len % MOBA_BLOCK == 0 and t_new == 4
    n_p = batch * t_len
    n_s = nb * t_new

    pos_p = jnp.tile(jnp.arange(t_len, dtype=jnp.int32), batch)
    pos_s = jnp.tile(past_len + jnp.arange(t_new, dtype=jnp.int32), nb)
    cos_p, sin_p = _rope_tables(pos_p)
    cos_s, sin_s = _rope_tables(pos_s)

    kv_pages = lambda c: _native_pages(c, depth, n_pool).reshape(
        depth * n_pool, 2, N_KV_HEADS * HEAD_DIM, page_size)
    ca_kv, cb_kv, cc_kv = kv_pages(cache_a_kv), kv_pages(cache_b_kv), kv_pages(cache_c_kv)
    ca_idx = _native_pages(cache_a_idx, depth, n_pool)
    cc_lft = _native_pages(cache_c_logf, depth, n_pool)

    xp = x_prompt.reshape(n_p, D_MODEL)
    xs = x_sample.reshape(n_s, D_MODEL)
    gfin = final_norm_g.reshape(1, D_MODEL)
    rows_p = []
    rows_s = []
    for l in range(depth):
        w = _layout_w_in(w_in[l])
        g = norm_g[l].reshape(1, D_MODEL)
        bf_pad = jnp.zeros((1, LANES), F32).at[0, MISC_F:MISC_F + N_HEADS].set(b_forget[l])
        wbr = w_branch[l].astype(BF16)
        wout = w_out[l].astype(BF16)
        final = l == depth - 1

        proj_p = _proj_call(xp, g, w, cos_p, sin_p, bf_pad, 256)
        rows_p.append(proj_p[8:13])
        o_a, o_b, o_c = _prompt_branches(proj_p, batch, t_len)
        xp = _merge_call(xp, o_a, o_b, o_c, proj_p[14], proj_p[15], wbr, wout, gfin, 256, final)

        proj_s = _proj_call(xs, g, w, cos_s, sin_s, bf_pad, n_s)
        rows_s.append(proj_s[8:13])
        o_a, o_b, o_c = _sample_branches(proj_s, nb, t_new, page_table, l * n_pool,
                                         (ca_idx, ca_kv, cb_kv, cc_kv, cc_lft))
        xs = _merge_call(xs, o_a, o_b, o_c, proj_s[14], proj_s[15], wbr, wout, gfin, n_s, final)

    def stack(rows, i, b, t, tail):
        return jnp.stack([r[i] for r in rows]).reshape((depth, b, t) + tail)

    kv_tail = (2, N_KV_HEADS, HEAD_DIM)
    outs_p = (stack(rows_p, 0, batch, t_len, kv_tail), stack(rows_p, 1, batch, t_len, (IDX_DIM,)),
              stack(rows_p, 2, batch, t_len, kv_tail), stack(rows_p, 3, batch, t_len, kv_tail),
              stack(rows_p, 4, batch, t_len, (N_HEADS,)))
    outs_s = (stack(rows_s, 0, nb, t_new, kv_tail), stack(rows_s, 1, nb, t_new, (IDX_DIM,)),
              stack(rows_s, 2, nb, t_new, kv_tail), stack(rows_s, 3, nb, t_new, kv_tail),
              stack(rows_s, 4, nb, t_new, (N_HEADS,)))
    return (xp.reshape(batch, t_len, D_MODEL), xs.reshape(nb, t_new, D_MODEL)) + outs_p + outs_s
```
